```python
import jax, jax.numpy as jnp
from jax import lax
import numpy as np

D_MODEL = 1024
BATCH = 16
SEQ = 2048
DEPTH = 2
DEC_BATCH = 128
DEC_SEQ = 1
PAST_LEN = 8192
PAGE_SIZE = 128

N_MIXERS = 2
N_MLA_LAYERS = (DEPTH + 1) // 2
N_POOL_LAYERS = DEPTH // 2
EPS = 1e-6
N_HEADS = 8
QK_NOPE_DIM = 128
QK_ROPE_DIM = 64
V_HEAD_DIM = 128
Q_LORA_RANK = 384
KV_LORA_RANK = 256
CACHE_ROW = KV_LORA_RANK + QK_ROPE_DIM
ROPE_THETA = 10000.0
ATTN_SCALE = (QK_NOPE_DIM + QK_ROPE_DIM) ** -0.5
Q_BLOCK = 128
POOL_WINDOWS = (2, 4, 8, 16)
POOL_GROUPS = len(POOL_WINDOWS)
POOL_GROUP_DIM = D_MODEL // POOL_GROUPS
POOL_STATE_LEN = max(POOL_WINDOWS) - 1
PEER_HEADS = 8
PEER_N_KEYS = 128
PEER_N_EXPERTS = PEER_N_KEYS * PEER_N_KEYS
PEER_TOPK = 16
PEER_QUERY_DIM = 256
PEER_HALF = PEER_QUERY_DIM // 2
PEER_BLOCK = 128
PAGE_POOL_NUM = 5
PAGE_POOL_DEN = 4

kernel_name = 'hybrid_mla_pool_peer_adaln_step'


def rmsnorm(x, g):
    xf = x.astype(jnp.float32)
    y = xf * lax.rsqrt(jnp.mean(xf * xf, axis=-1, keepdims=True) + EPS)
    return (y * g.astype(jnp.float32)).astype(x.dtype)


def modulate(h, shift, scale):
    return h * (1 + scale[:, None, :]) + shift[:, None, :]


def adaln(c, w, b):
    return jnp.split(jax.nn.silu(c) @ w + b, 6, axis=-1)


def rope_tables(pos):
    inv = ROPE_THETA ** (-jnp.arange(0, QK_ROPE_DIM, 2, dtype=jnp.float32) / QK_ROPE_DIM)
    ang = pos.astype(jnp.float32)[:, None] * inv[None, :]
    return jnp.cos(ang), jnp.sin(ang)


def apply_rope(x, cos, sin):
    xf = x.astype(jnp.float32)
    x1, x2 = xf[..., :QK_ROPE_DIM // 2], xf[..., QK_ROPE_DIM // 2:]
    return jnp.concatenate([x1 * cos - x2 * sin, x1 * sin + x2 * cos], -1).astype(x.dtype)


def mla_project(h, pos, w_in, g_q_lat, g_kv_lat, w_uq, g_qn_nope, g_qn_rope, g_kn_rope):
    b, t, _ = h.shape
    z = h @ w_in
    cq = rmsnorm(z[..., :Q_LORA_RANK], g_q_lat)
    lat = rmsnorm(z[..., Q_LORA_RANK:Q_LORA_RANK + KV_LORA_RANK], g_kv_lat)
    cos, sin = rope_tables(pos)
    kr = apply_rope(rmsnorm(z[..., Q_LORA_RANK + KV_LORA_RANK:], g_kn_rope), cos, sin)
    q = (cq @ w_uq).reshape(b, t, N_HEADS, QK_NOPE_DIM + QK_ROPE_DIM)
    qn = rmsnorm(q[..., :QK_NOPE_DIM], g_qn_nope)
    qr = apply_rope(rmsnorm(q[..., QK_NOPE_DIM:], g_qn_rope), cos[:, None], sin[:, None])
    return qn, qr, lat, kr


def key_nope(lat, w_uk, g_kn_nope):
    return rmsnorm(jnp.einsum('...c,chd->...hd', lat, w_uk), g_kn_nope)


def mla_attend(qn, qr, kn, kr, lat, q_pos, k_pos, w_uv):
    s = (jnp.einsum('bqhd,bkhd->bhqk', qn, kn, preferred_element_type=jnp.float32)
         + jnp.einsum('bqhr,bkr->bhqk', qr, kr, preferred_element_type=jnp.float32)) * ATTN_SCALE
    s = jnp.where(k_pos[None, None, None, :] <= q_pos[None, None, :, None], s, -jnp.inf)
    p = jax.nn.softmax(s, axis=-1).astype(lat.dtype)
    ctx = jnp.einsum('bhqk,bkc->bqhc', p, lat)
    return jnp.einsum('bqhc,chv->bqhv', ctx, w_uv)


def mla_prompt(h, pos, w_in, g_q_lat, g_kv_lat, w_uq, w_uk, w_uv,
               g_qn_nope, g_qn_rope, g_kn_nope, g_kn_rope, w_out):
    b, s, _ = h.shape
    qn, qr, lat, kr = mla_project(h, pos, w_in, g_q_lat, g_kv_lat, w_uq, g_qn_nope, g_qn_rope, g_kn_rope)
    kn = key_nope(lat, w_uk, g_kn_nope)
    nqb = s // Q_BLOCK
    qn_b = qn.reshape(b, nqb, Q_BLOCK, N_HEADS, QK_NOPE_DIM).transpose(1, 0, 2, 3, 4)
    qr_b = qr.reshape(b, nqb, Q_BLOCK, N_HEADS, QK_ROPE_DIM).transpose(1, 0, 2, 3, 4)
    qp_b = pos.reshape(nqb, Q_BLOCK)

    def block(args):
        qn_i, qr_i, qp_i = args
        return mla_attend(qn_i, qr_i, kn, kr, lat, qp_i, pos, w_uv)

    o = lax.map(block, (qn_b, qr_b, qp_b))
    o = o.transpose(1, 0, 2, 3, 4).reshape(b, s, N_HEADS * V_HEAD_DIM)
    return o @ w_out, jnp.concatenate([lat, kr], -1)


def mla_sample(h, pos, cache, page_table, w_in, g_q_lat, g_kv_lat, w_uq, w_uk, w_uv,
               g_qn_nope, g_qn_rope, g_kn_nope, g_kn_rope, w_out):
    b, t, _ = h.shape
    qn, qr, lat, kr = mla_project(h, pos, w_in, g_q_lat, g_kv_lat, w_uq, g_qn_nope, g_qn_rope, g_kn_rope)
    new_rows = jnp.concatenate([lat, kr], -1)

    def one_seq(args):
        qn_i, qr_i, pages, rows_i = args
        past = cache[pages].reshape(-1, CACHE_ROW)
        rows = jnp.concatenate([past, rows_i.astype(past.dtype)], 0)
        lat_i, kr_i = rows[:, :KV_LORA_RANK], rows[:, KV_LORA_RANK:]
        kn_i = key_nope(lat_i, w_uk, g_kn_nope)
        k_pos = jnp.arange(rows.shape[0])
        return mla_attend(qn_i[None], qr_i[None], kn_i[None], kr_i[None], lat_i[None],
                          pos, k_pos, w_uv)[0]

    o = lax.map(one_seq, (qn, qr, page_table, new_rows))
    return o.reshape(b, t, N_HEADS * V_HEAD_DIM) @ w_out, new_rows


def pool_mixer(h, prefix, pos, w_pool, scale):
    b, t, _ = h.shape
    p = prefix.shape[1]
    ext = jnp.concatenate([prefix.astype(h.dtype), h], 1)
    extf = ext.astype(jnp.float32)
    cs = jnp.concatenate([jnp.zeros((b, 1, D_MODEL), jnp.float32), jnp.cumsum(extf, axis=1)], 1)
    end = cs[:, p + 1:p + 1 + t]
    parts = []
    for gi, w in enumerate(POOL_WINDOWS):
        ch = slice(gi * POOL_GROUP_DIM, (gi + 1) * POOL_GROUP_DIM)
        start = cs[:, p + 1 - w:p + 1 - w + t, ch]
        cnt = jnp.minimum(pos + 1, w).astype(jnp.float32)[None, :, None]
        parts.append((end[..., ch] - start) / cnt)
    d = (jnp.concatenate(parts, -1) - extf[:, p:]).astype(h.dtype)
    d = d.reshape(b, t, POOL_GROUPS, POOL_GROUP_DIM)
    y = jnp.einsum('btgc,gce->btge', d, w_pool).reshape(b, t, D_MODEL) * scale
    return y, ext[:, -POOL_STATE_LEN:]


def peer(h, w_q, subkeys, u, v):
    shp = h.shape
    flat = h.reshape(-1, D_MODEL)
    n = flat.shape[0]
    pad = (-n) % PEER_BLOCK
    blocks = jnp.pad(flat, ((0, pad), (0, 0))).reshape(-1, PEER_BLOCK, D_MODEL)

    def block(xb):
        q = (xb @ w_q).reshape(PEER_BLOCK, PEER_HEADS, 2, PEER_HALF)
        s = jnp.einsum('thpc,hpnc->thpn', q, subkeys).astype(jnp.float32)
        vals, idx = lax.top_k(s, PEER_TOPK)
        cand = (vals[:, :, 0, :, None] + vals[:, :, 1, None, :]).reshape(
            PEER_BLOCK, PEER_HEADS, PEER_TOPK * PEER_TOPK)
        best, ci = lax.top_k(cand, PEER_TOPK)
        i1 = jnp.take_along_axis(idx[:, :, 0], ci // PEER_TOPK, axis=-1)
        i2 = jnp.take_along_axis(idx[:, :, 1], ci % PEER_TOPK, axis=-1)
        e = (i1 * PEER_N_KEYS + i2).reshape(PEER_BLOCK, PEER_HEADS * PEER_TOPK)
        g = jax.nn.softmax(best, axis=-1).reshape(PEER_BLOCK, PEER_HEADS * PEER_TOPK)
        a = jax.nn.gelu(jnp.einsum('td,tkd->tk', xb, u[e]), approximate=False)
        return jnp.einsum('tk,tkd->td', (g * a).astype(xb.dtype), v[e])

    y = lax.map(block, blocks).reshape(-1, D_MODEL)[:n]
    return y.reshape(shp)


def setup_inputs(seed: int = 0) -> dict:
    key = jax.random.key(seed)
    ks = jax.random.split(key, 28)
    f32 = jnp.float32

    def nrm(k, shape, s):
        return jax.random.normal(k, shape, f32) * s

    def gain(k, shape):
        return 1.0 + 0.05 * jax.random.normal(k, shape, f32)

    n_pages = PAST_LEN // PAGE_SIZE
    n_used = DEC_BATCH * n_pages
    n_phys = (n_used * PAGE_POOL_NUM + PAGE_POOL_DEN - 1) // PAGE_POOL_DEN
    page_table = jax.random.permutation(ks[6], n_phys)[:n_used].reshape(DEC_BATCH, n_pages).astype(jnp.int32)
    L, A, P = DEPTH, N_MLA_LAYERS, N_POOL_LAYERS
    qk = QK_NOPE_DIM + QK_ROPE_DIM
    return {
        'x_prompt': nrm(ks[0], (BATCH, SEQ, D_MODEL), 1.0),
        'x_sample': nrm(ks[1], (DEC_BATCH, DEC_SEQ, D_MODEL), 1.0),
        'c_prompt': nrm(ks[2], (BATCH, D_MODEL), 1.0),
        'c_sample': nrm(ks[3], (DEC_BATCH, D_MODEL), 1.0),
        'cache_mla': nrm(ks[4], (A, n_phys, PAGE_SIZE, CACHE_ROW), 1.0),
        'state_pool': nrm(ks[5], (P, DEC_BATCH, POOL_STATE_LEN, D_MODEL), 1.0),
        'page_table': page_table,
        'norm_mix': gain(ks[7], (L, D_MODEL)),
        'norm_ffn': gain(ks[8], (L, D_MODEL)),
        'w_ada': nrm(ks[9], (L, D_MODEL, 6 * D_MODEL), 0.5 * D_MODEL ** -0.5),
        'b_ada': nrm(ks[10], (L, 6 * D_MODEL), 0.02),
        'w_mla_in': nrm(ks[11], (A, D_MODEL, Q_LORA_RANK + KV_LORA_RANK + QK_ROPE_DIM), D_MODEL ** -0.5),
        'g_q_lat': gain(ks[12], (A, Q_LORA_RANK)),
        'g_kv_lat': gain(ks[13], (A, KV_LORA_RANK)),
        'w_uq': nrm(ks[14], (A, Q_LORA_RANK, N_HEADS * qk), Q_LORA_RANK ** -0.5),
        'w_uk': nrm(ks[15], (A, KV_LORA_RANK, N_HEADS, QK_NOPE_DIM), KV_LORA_RANK ** -0.5),
        'w_uv': nrm(ks[16], (A, KV_LORA_RANK, N_HEADS, V_HEAD_DIM), KV_LORA_RANK ** -0.5),
        'g_qn_nope': gain(ks[17], (A, QK_NOPE_DIM)),
        'g_qn_rope': gain(ks[18], (A, QK_ROPE_DIM)),
        'g_kn_nope': gain(ks[19], (A, QK_NOPE_DIM)),
        'g_kn_rope': gain(ks[20], (A, QK_ROPE_DIM)),
        'w_mla_out': nrm(ks[21], (A, N_HEADS * V_HEAD_DIM, D_MODEL), (N_HEADS * V_HEAD_DIM) ** -0.5),
        'w_pool': nrm(ks[22], (P, POOL_GROUPS, POOL_GROUP_DIM, POOL_GROUP_DIM), POOL_GROUP_DIM ** -0.5),
        'pool_scale': gain(ks[23], (P, D_MODEL)),
        'w_peer_q': nrm(ks[24], (L, D_MODEL, PEER_HEADS * PEER_QUERY_DIM), D_MODEL ** -0.5),
        'peer_subkeys': nrm(ks[25], (L, PEER_HEADS, 2, PEER_N_KEYS, PEER_HALF), PEER_HALF ** -0.5),
        'peer_u': nrm(ks[26], (L, PEER_N_EXPERTS, D_MODEL), D_MODEL ** -0.5),
        'peer_v': nrm(ks[27], (L, PEER_N_EXPERTS, D_MODEL), PEER_HEADS ** -0.5),
    }


def reference(x_prompt, x_sample, c_prompt, c_sample, cache_mla, state_pool, page_table,
              norm_mix, norm_ffn, w_ada, b_ada, w_mla_in, g_q_lat, g_kv_lat, w_uq, w_uk, w_uv,
              g_qn_nope, g_qn_rope, g_kn_nope, g_kn_rope, w_mla_out, w_pool, pool_scale,
              w_peer_q, peer_subkeys, peer_u, peer_v):
    pos_p = jnp.arange(x_prompt.shape[1])
    pos_s = PAST_LEN + jnp.arange(x_sample.shape[1])
    xp, xs = x_prompt, x_sample
    rows_p, rows_s, pst_p, pst_s = [], [], [], []
    for l in range(DEPTH):
        sp = adaln(c_prompt, w_ada[l], b_ada[l])
        ss = adaln(c_sample, w_ada[l], b_ada[l])
        hp = modulate(rmsnorm(xp, norm_mix[l]), sp[0], sp[1])
        hs = modulate(rmsnorm(xs, norm_mix[l]), ss[0], ss[1])
        i = l // N_MIXERS
        if l % N_MIXERS == 0:
            mw = (w_mla_in[i], g_q_lat[i], g_kv_lat[i], w_uq[i], w_uk[i], w_uv[i],
                  g_qn_nope[i], g_qn_rope[i], g_kn_nope[i], g_kn_rope[i], w_mla_out[i])
            mp, r_p = mla_prompt(hp, pos_p, *mw)
            ms, r_s = mla_sample(hs, pos_s, cache_mla[i], page_table, *mw)
            rows_p.append(r_p)
            rows_s.append(r_s)
        else:
            zero_prefix = jnp.zeros((hp.shape[0], POOL_STATE_LEN, D_MODEL), hp.dtype)
            mp, st_p = pool_mixer(hp, zero_prefix, pos_p, w_pool[i], pool_scale[i])
            ms, st_s = pool_mixer(hs, state_pool[i], pos_s, w_pool[i], pool_scale[i])
            pst_p.append(st_p)
            pst_s.append(st_s)
        xp = xp + sp[2][:, None, :] * mp
        xs = xs + ss[2][:, None, :] * ms
        hp = modulate(rmsnorm(xp, norm_ffn[l]), sp[3], sp[4])
        hs = modulate(rmsnorm(xs, norm_ffn[l]), ss[3], ss[4])
        xp = xp + sp[5][:, None, :] * peer(hp, w_peer_q[l], peer_subkeys[l], peer_u[l], peer_v[l])
        xs = xs + ss[5][:, None, :] * peer(hs, w_peer_q[l], peer_subkeys[l], peer_u[l], peer_v[l])
    return (xp, xs, jnp.stack(rows_p), jnp.stack(rows_s), jnp.stack(pst_p), jnp.stack(pst_s))
```

```python
import functools

import jax
import jax.numpy as jnp
from jax import lax
from jax.experimental import pallas as pl
from jax.experimental.pallas import tpu as pltpu

F32 = jnp.float32
BF16 = jnp.bfloat16

D_MODEL = 1024
EPS = 1e-6
PAST_LEN = 8192
PAGE_SIZE = 128
N_HEADS = 8
QK_NOPE_DIM = 128
QK_ROPE_DIM = 64
ROPE_HALF = QK_ROPE_DIM // 2
ROPE_PAD = 128
HEAD_PAD = QK_NOPE_DIM + ROPE_PAD
V_HEAD_DIM = 128
Q_LORA_RANK = 384
KV_LORA_RANK = 256
CACHE_ROW = KV_LORA_RANK + QK_ROPE_DIM
ROPE_THETA = 10000.0
ATTN_SCALE = (QK_NOPE_DIM + QK_ROPE_DIM) ** -0.5
POOL_WINDOWS = (2, 4, 8, 16)
POOL_GROUP_DIM = D_MODEL // len(POOL_WINDOWS)
POOL_STATE_LEN = max(POOL_WINDOWS) - 1
POOL_HALO = 16
PEER_HEADS = 8
PEER_N_KEYS = 128
PEER_N_EXPERTS = PEER_N_KEYS * PEER_N_KEYS
PEER_TOPK = 16
PEER_HALF = 128
EXPERT_CHUNK = 2048
KEY_ROWS_PER_CHUNK = EXPERT_CHUNK // PEER_N_KEYS

VMEM_LIMIT = 56 * 1024 * 1024

NEG_INF = float("-inf")


def _cparams(n_grid):
    return pltpu.CompilerParams(
        dimension_semantics=("arbitrary",) * n_grid, vmem_limit_bytes=VMEM_LIMIT)


def _rms(x, g, n=None):
    n = x.shape[-1] if n is None else n
    ms = jnp.sum(x * x, axis=-1, keepdims=True) * (1.0 / n)
    return x * lax.rsqrt(ms + EPS) * g


def _gelu(x):
    return 0.5 * x * (1.0 + lax.erf(x * (2.0 ** -0.5)))


def _bdot(a, b):
    return jnp.dot(a.astype(BF16), b.astype(BF16), preferred_element_type=F32)


def _bdot_nt(a, b):
    return lax.dot_general(a.astype(BF16), b.astype(BF16), (((1,), (1,)), ((), ())),
                           preferred_element_type=F32)


def _mods_kernel(c_ref, w_ref, b_ref, o_ref):
    o_ref[0] = _bdot(jax.nn.silu(c_ref[...]), w_ref[0]) + b_ref[0]


def _ada_params(c_all, w_ada, b_ada):
    n_layers = w_ada.shape[0]
    n_rows = c_all.shape[0]
    n_col = w_ada.shape[2] // D_MODEL
    return pl.pallas_call(
        _mods_kernel,
        grid=(n_layers, n_col),
        in_specs=[
            pl.BlockSpec((n_rows, D_MODEL), lambda l, k: (0, 0)),
            pl.BlockSpec((1, D_MODEL, D_MODEL), lambda l, k: (l, 0, k)),
            pl.BlockSpec((1, 1, D_MODEL), lambda l, k: (l, 0, k)),
        ],
        out_specs=pl.BlockSpec((1, n_rows, D_MODEL), lambda l, k: (l, 0, k)),
        out_shape=jax.ShapeDtypeStruct((n_layers, n_rows, n_col * D_MODEL), F32),
        compiler_params=_cparams(2),
        name="ada_params",
    )(c_all, w_ada, b_ada.reshape(n_layers, 1, -1))


def _rope_pad(x, g, cs, sn):
    y = _rms(x, g, QK_ROPE_DIM)
    return y * cs + pltpu.roll(y, ROPE_PAD // 2, axis=1) * sn


def _mla_proj_kernel(absorb, x_ref, shift_ref, scale_ref, gmix_ref, cs_ref, sn_ref, w_in_ref,
                     g_q_lat_ref, g_kv_lat_ref, g_kr_ref, w_uq_ref, g_qn_ref, g_qr_ref,
                     w_uk_ref, g_kn_ref, *rest):
    if absorb:
        w_ukt_ref, rows_ref, q_ref, k_ref, v_ref, qa_ref, sself_ref = rest
    else:
        rows_ref, q_ref, k_ref, v_ref = rest
    cs = cs_ref[...]
    sn = sn_ref[...]
    h = _rms(x_ref[0], gmix_ref[...]) * (1.0 + scale_ref[0]) + shift_ref[0]
    z = _bdot(h, w_in_ref[...])
    cq = _rms(z[:, :Q_LORA_RANK], g_q_lat_ref[...])
    lat = _rms(z[:, Q_LORA_RANK:Q_LORA_RANK + KV_LORA_RANK], g_kv_lat_ref[...])
    kr = _rope_pad(z[:, Q_LORA_RANK + KV_LORA_RANK:], g_kr_ref[...], cs, sn)
    rows_ref[0, :, :KV_LORA_RANK] = lat
    rows_ref[0, :, KV_LORA_RANK:] = (kr + pltpu.roll(kr, ROPE_PAD - ROPE_HALF, axis=1))[:, :QK_ROPE_DIM]
    v_ref[0] = lat.astype(BF16)
    q = _bdot(cq, w_uq_ref[...])
    kn_all = _bdot(lat, w_uk_ref[...])
    kr_b = kr.astype(BF16)
    if absorb:
        lane = lax.broadcasted_iota(jnp.int32, (x_ref.shape[1], 128), 1)
        sself = jnp.zeros((x_ref.shape[1], 128), F32)
    for hd in range(N_HEADS):
        o = hd * HEAD_PAD
        qn = _rms(q[:, o:o + QK_NOPE_DIM], g_qn_ref[...]) * ATTN_SCALE
        qr = _rope_pad(q[:, o + QK_NOPE_DIM:o + HEAD_PAD], g_qr_ref[...], cs, sn) * ATTN_SCALE
        kn = _rms(kn_all[:, hd * QK_NOPE_DIM:(hd + 1) * QK_NOPE_DIM], g_kn_ref[...])
        qn_b, qr_b, kn_b = qn.astype(BF16), qr.astype(BF16), kn.astype(BF16)
        q_ref[0, :, o:o + QK_NOPE_DIM] = qn_b
        q_ref[0, :, o + QK_NOPE_DIM:o + HEAD_PAD] = qr_b
        k_ref[0, :, o:o + QK_NOPE_DIM] = kn_b
        k_ref[0, :, o + QK_NOPE_DIM:o + HEAD_PAD] = kr_b
        if absorb:
            s = (jnp.sum(qn_b.astype(F32) * kn_b.astype(F32), axis=-1, keepdims=True)
                 + jnp.sum(qr_b.astype(F32) * kr_b.astype(F32), axis=-1, keepdims=True))
            sself = jnp.where(lane == hd, s, sself)
            qa = _bdot(qn * g_kn_ref[...], w_ukt_ref[hd])
            qa_ref[0, :, hd * KV_LORA_RANK:(hd + 1) * KV_LORA_RANK] = qa.astype(BF16)
    if absorb:
        sself_ref[0] = sself


def _mla_project(x, shift, scale, gmix, cs, sn, mw, tile, absorb):
    nb, s, _ = x.shape
    r = shift.shape[1]
    ns = s // tile
    rt = tile if r == s else 1
    tok = lambda bs: pl.BlockSpec(bs, lambda b, i: (b, i, 0))
    mod = pl.BlockSpec((1, rt, D_MODEL), (lambda b, i: (b, i, 0)) if r == s else (lambda b, i: (b, 0, 0)))
    full = lambda a: pl.BlockSpec(a.shape, lambda b, i: (0,) * a.ndim)
    weights = [mw["w_in"], mw["g_q_lat"], mw["g_kv_lat"], mw["g_kr"], mw["w_uq"], mw["g_qn"],
               mw["g_qr"], mw["w_uk"], mw["g_kn"]]
    if absorb:
        weights.append(mw["w_ukt"])
    out_shape = [jax.ShapeDtypeStruct((nb, s, CACHE_ROW), F32),
                 jax.ShapeDtypeStruct((nb, s, N_HEADS * HEAD_PAD), BF16),
                 jax.ShapeDtypeStruct((nb, s, N_HEADS * HEAD_PAD), BF16),
                 jax.ShapeDtypeStruct((nb, s, KV_LORA_RANK), BF16)]
    out_specs = [tok((1, tile, CACHE_ROW)), tok((1, tile, N_HEADS * HEAD_PAD)),
                 tok((1, tile, N_HEADS * HEAD_PAD)), tok((1, tile, KV_LORA_RANK))]
    if absorb:
        out_shape += [jax.ShapeDtypeStruct((nb, s, N_HEADS * KV_LORA_RANK), BF16),
                      jax.ShapeDtypeStruct((nb, s, 128), F32)]
        out_specs += [tok((1, tile, N_HEADS * KV_LORA_RANK)), tok((1, tile, 128))]
    return pl.pallas_call(
        functools.partial(_mla_proj_kernel, absorb),
        grid=(nb, ns),
        in_specs=[tok((1, tile, D_MODEL)), mod, mod, full(gmix),
                  pl.BlockSpec((tile, ROPE_PAD), lambda b, i: (i, 0)),
                  pl.BlockSpec((tile, ROPE_PAD), lambda b, i: (i, 0))] + [full(w) for w in weights],
        out_specs=out_specs,
        out_shape=out_shape,
        compiler_params=_cparams(2),
        name="mla_project_sample" if absorb else "mla_project",
    )(x, shift, scale, gmix, cs, sn, *weights)


def _attn_out(ctx_of_head, x, gate, w_uv_ref, w_out_ref):
    o = jnp.concatenate([_bdot(ctx_of_head(hd), w_uv_ref[hd]) for hd in range(N_HEADS)], axis=-1)
    return x + gate * _bdot(o, w_out_ref[...])


def _flash_kernel(q_ref, k_ref, v_ref, x_ref, gate_ref, w_uv_ref, w_out_ref, o_ref,
                  m_s, l_s, acc_s):
    qi = pl.program_id(1)
    ki = pl.program_id(2)
    tq = q_ref.shape[1]

    @pl.when(ki == 0)
    def _():
        m_s[...] = jnp.full(m_s.shape, NEG_INF, F32)
        l_s[...] = jnp.zeros(l_s.shape, F32)
        acc_s[...] = jnp.zeros(acc_s.shape, F32)

    def step(masked):
        v = v_ref[0]
        if masked:
            row = lax.broadcasted_iota(jnp.int32, (tq, tq), 0)
            col = lax.broadcasted_iota(jnp.int32, (tq, tq), 1)
            keep = col <= row
        for hd in range(N_HEADS):
            o = hd * HEAD_PAD
            s = _bdot_nt(q_ref[0, :, o:o + HEAD_PAD], k_ref[0, :, o:o + HEAD_PAD])
            if masked:
                s = jnp.where(keep, s, NEG_INF)
            m_prev = m_s[hd]
            m_new = jnp.maximum(m_prev, jnp.max(s, axis=-1, keepdims=True))
            alpha = jnp.exp(m_prev - m_new)
            p = jnp.exp(s - m_new)
            l_s[hd] = alpha * l_s[hd] + jnp.sum(p, axis=-1, keepdims=True)
            acc_s[hd] = alpha * acc_s[hd] + _bdot(p, v)
            m_s[hd] = m_new

    @pl.when(ki < qi)
    def _():
        step(False)

    @pl.when(ki == qi)
    def _():
        step(True)
        o_ref[0] = _attn_out(lambda hd: acc_s[hd] / l_s[hd], x_ref[0], gate_ref[0],
                             w_uv_ref, w_out_ref)


def _prompt_attention(q, k, v, x, gate, w_uv, w_out, tile):
    nb, s, _ = x.shape
    nt = s // tile
    full = lambda a: pl.BlockSpec(a.shape, lambda b, i, j: (0,) * a.ndim)
    return pl.pallas_call(
        _flash_kernel,
        grid=(nb, nt, nt),
        in_specs=[
            pl.BlockSpec((1, tile, N_HEADS * HEAD_PAD), lambda b, i, j: (b, i, 0)),
            pl.BlockSpec((1, tile, N_HEADS * HEAD_PAD), lambda b, i, j: (b, jnp.minimum(i, j), 0)),
            pl.BlockSpec((1, tile, KV_LORA_RANK), lambda b, i, j: (b, jnp.minimum(i, j), 0)),
            pl.BlockSpec((1, tile, D_MODEL), lambda b, i, j: (b, i, 0)),
            pl.BlockSpec((1, 1, D_MODEL), lambda b, i, j: (b, 0, 0)),
            full(w_uv), full(w_out),
        ],
        out_specs=pl.BlockSpec((1, tile, D_MODEL), lambda b, i, j: (b, i, 0)),
        out_shape=jax.ShapeDtypeStruct(x.shape, F32),
        scratch_shapes=[pltpu.VMEM((N_HEADS, tile, 1), F32), pltpu.VMEM((N_HEADS, tile, 1), F32),
                        pltpu.VMEM((N_HEADS, tile, KV_LORA_RANK), F32)],
        compiler_params=_cparams(3),
        name="prompt_attention",
    )(q, k, v, x, gate, w_uv, w_out)


PAGES_PER_STEP = 16
PAGES_PER_DOT = 4


def _sample_attn_kernel(pt_ref, qa_ref, qr_ref, sself_ref, lat_new_ref, hsel_ref, w_uk_ref, *rest):
    del pt_ref
    pages = rest[:PAGES_PER_STEP]
    ctx_ref, m_s, l_s, acc_s = rest[PAGES_PER_STEP:]
    step = pl.program_id(1)

    @pl.when(step == 0)
    def _():
        m_s[...] = sself_ref[0]
        l_s[...] = jnp.ones(l_s.shape, F32)
        acc_s[...] = jnp.broadcast_to(lat_new_ref[0], acc_s.shape)

    qa = qa_ref[0]
    qr = qr_ref[0]
    for c in range(PAGES_PER_STEP // PAGES_PER_DOT):
        rows = jnp.concatenate([pages[c * PAGES_PER_DOT + p][...] for p in range(PAGES_PER_DOT)], axis=0)
        lat = rows[:, :KV_LORA_RANK].astype(BF16)
        kr = rows[:, KV_LORA_RANK:].astype(BF16)
        y = _bdot(lat, w_uk_ref[...])
        ssq = _bdot_nt(hsel_ref[...], y * y)
        rinv = lax.rsqrt(ssq * (1.0 / QK_NOPE_DIM) + EPS)
        s = _bdot_nt(qa, lat) * rinv + _bdot_nt(qr, kr)
        m_prev = m_s[...]
        m_new = jnp.maximum(m_prev, jnp.max(s, axis=-1, keepdims=True))
        alpha = jnp.exp(m_prev - m_new)
        p = jnp.exp(s - m_new)
        l_s[...] = alpha * l_s[...] + jnp.sum(p, axis=-1, keepdims=True)
        acc_s[...] = alpha * acc_s[...] + _bdot(p, lat)
        m_s[...] = m_new

    @pl.when(step == pl.num_programs(1) - 1)
    def _():
        ctx_ref[0] = acc_s[...] / l_s[...]


def _sample_attention(cache, page_table, qa, qr, sself, lat_new, hsel, w_uk):
    nb, n_pages = page_table.shape
    n_steps = n_pages // PAGES_PER_STEP
    per_b = lambda a: pl.BlockSpec((1,) + a.shape[1:], lambda b, s, pt: (b,) + (0,) * (a.ndim - 1))
    full = lambda a: pl.BlockSpec(a.shape, lambda b, s, pt: (0,) * a.ndim)

    def page_spec(j):
        return pl.BlockSpec((None, PAGE_SIZE, CACHE_ROW),
                            lambda b, s, pt: (pt[b * n_pages + s * PAGES_PER_STEP + j], 0, 0))

    grid_spec = pltpu.PrefetchScalarGridSpec(
        num_scalar_prefetch=1,
        grid=(nb, n_steps),
        in_specs=[per_b(qa), per_b(qr), per_b(sself), per_b(lat_new), full(hsel), full(w_uk)]
        + [page_spec(j) for j in range(PAGES_PER_STEP)],
        out_specs=pl.BlockSpec((1, N_HEADS, KV_LORA_RANK), lambda b, s, pt: (b, 0, 0)),
        scratch_shapes=[pltpu.VMEM((N_HEADS, 1), F32), pltpu.VMEM((N_HEADS, 1), F32),
                        pltpu.VMEM((N_HEADS, KV_LORA_RANK), F32)],
    )
    return pl.pallas_call(
        _sample_attn_kernel,
        grid_spec=grid_spec,
        out_shape=jax.ShapeDtypeStruct((nb, N_HEADS, KV_LORA_RANK), F32),
        compiler_params=_cparams(2),
        name="sample_attention",
    )(page_table.reshape(-1), qa, qr, sself, lat_new, hsel, w_uk, *([cache] * PAGES_PER_STEP))


def _sample_out_kernel(ctx_ref, x_ref, gate_ref, w_uv_ref, w_out_ref, o_ref):
    o_ref[...] = _attn_out(lambda hd: ctx_ref[:, hd * KV_LORA_RANK:(hd + 1) * KV_LORA_RANK],
                           x_ref[...], gate_ref[...], w_uv_ref, w_out_ref)


def _sample_out(ctx, x, gate, w_uv, w_out):
    return pl.pallas_call(
        _sample_out_kernel,
        out_shape=jax.ShapeDtypeStruct(x.shape, F32),
        compiler_params=pltpu.CompilerParams(vmem_limit_bytes=VMEM_LIMIT),
        name="sample_attention_out",
    )(ctx, x, gate, w_uv, w_out)


def _pool_mix(window_sum, h, cnt, w_pool_ref, pscale):
    parts = []
    for gi, w in enumerate(POOL_WINDOWS):
        ch = slice(gi * POOL_GROUP_DIM, (gi + 1) * POOL_GROUP_DIM)
        acc = h[:, ch]
        for kk in range(1, w):
            acc = acc + window_sum(kk, ch)
        d = acc / cnt[gi] - h[:, ch]
        parts.append(_bdot(d, w_pool_ref[gi]))
    return jnp.concatenate(parts, axis=-1) * pscale


def _pool_prompt_kernel(x_ref, shift_ref, scale_ref, gate_ref, gmix_ref, w_pool_ref, pscale_ref,
                        o_ref, st_ref, ext_s):
    si = pl.program_id(1)
    t = x_ref.shape[1]

    @pl.when(si == 0)
    def _():
        ext_s[0:POOL_HALO] = jnp.zeros((POOL_HALO, D_MODEL), F32)

    @pl.when(si > 0)
    def _():
        ext_s[0:POOL_HALO] = ext_s[t:t + POOL_HALO]

    x = x_ref[0]
    h = _rms(x, gmix_ref[...]) * (1.0 + scale_ref[0]) + shift_ref[0]
    ext_s[POOL_HALO:POOL_HALO + t] = h
    pos = si * t + lax.broadcasted_iota(jnp.int32, (t, 1), 0)
    cnt = [jnp.minimum(pos + 1, w).astype(F32) for w in POOL_WINDOWS]
    y = _pool_mix(lambda kk, ch: ext_s[POOL_HALO - kk:POOL_HALO - kk + t, ch], h, cnt,
                  w_pool_ref, pscale_ref[...])
    o_ref[0] = x + gate_ref[0] * y

    @pl.when(si == pl.num_programs(1) - 1)
    def _():
        st_ref[0] = ext_s[POOL_HALO + t - POOL_STATE_LEN:POOL_HALO + t]


def _pool_prompt(x, shift, scale, gate, gmix, w_pool, pscale, tile):
    nb, s, _ = x.shape
    tok = pl.BlockSpec((1, tile, D_MODEL), lambda b, i: (b, i, 0))
    mod = pl.BlockSpec((1, 1, D_MODEL), lambda b, i: (b, 0, 0))
    full = lambda a: pl.BlockSpec(a.shape, lambda b, i: (0,) * a.ndim)
    return pl.pallas_call(
        _pool_prompt_kernel,
        grid=(nb, s // tile),
        in_specs=[tok, mod, mod, mod, full(gmix), full(w_pool), full(pscale)],
        out_specs=[tok, pl.BlockSpec((1, POOL_STATE_LEN, D_MODEL), lambda b, i: (b, 0, 0))],
        out_shape=[jax.ShapeDtypeStruct(x.shape, F32),
                   jax.ShapeDtypeStruct((nb, POOL_STATE_LEN, D_MODEL), F32)],
        scratch_shapes=[pltpu.VMEM((POOL_HALO + tile, D_MODEL), F32)],
        compiler_params=_cparams(2),
        name="pool_prompt",
    )(x, shift, scale, gate, gmix, w_pool, pscale)


def _pool_sample_kernel(x_ref, shift_ref, scale_ref, gate_ref, gmix_ref, w_pool_ref, pscale_ref,
                        state_ref, o_ref, st_ref):
    x = x_ref[...]
    h = _rms(x, gmix_ref[...]) * (1.0 + scale_ref[...]) + shift_ref[...]
    cnt = [float(w) for w in POOL_WINDOWS]
    y = _pool_mix(lambda kk, ch: state_ref[POOL_STATE_LEN - kk, :, ch], h, cnt,
                  w_pool_ref, pscale_ref[...])
    o_ref[...] = x + gate_ref[...] * y
    for r in range(POOL_STATE_LEN - 1):
        st_ref[r] = state_ref[r + 1]
    st_ref[POOL_STATE_LEN - 1] = h


def _pool_sample(x, shift, scale, gate, gmix, w_pool, pscale, state_t):
    return pl.pallas_call(
        _pool_sample_kernel,
        out_shape=[jax.ShapeDtypeStruct(x.shape, F32), jax.ShapeDtypeStruct(state_t.shape, F32)],
        compiler_params=pltpu.CompilerParams(vmem_limit_bytes=VMEM_LIMIT),
        name="pool_sample",
    )(x, shift, scale, gate, gmix, w_pool, pscale, state_t)


def _cand_rows(b):
    return PEER_TOPK // (b + 1)


def _kth_largest(work, k):
    vals = []
    for _ in range(k):
        m = jnp.max(work, axis=0, keepdims=True)
        vals.append(m)
        work = jnp.where(work == m, NEG_INF, work)
    return vals


def _peer_route_kernel(x_ref, shift_ref, scale_ref, gffn_ref, wqt_ref, subk_ref,
                       ht_ref, s_ref, e_ref, tau_ref):
    t = x_ref.shape[1]
    h = _rms(x_ref[0], gffn_ref[...]) * (1.0 + scale_ref[0]) + shift_ref[0]
    ht = h.T.astype(BF16)
    ht_ref[...] = ht
    qt = jnp.dot(wqt_ref[...], ht, preferred_element_type=F32).astype(BF16)
    row16 = lax.broadcasted_iota(jnp.int32, (PEER_TOPK, t), 0)
    row8 = lax.broadcasted_iota(jnp.int32, (8, t), 0)
    for hd in range(PEER_HEADS):
        sc, top = [], []
        for p in range(2):
            hp = 2 * hd + p
            s = jnp.dot(subk_ref[hp], qt[hp * PEER_HALF:(hp + 1) * PEER_HALF],
                        preferred_element_type=F32)
            s_ref[hp] = s
            sc.append(s)
            top.append(_kth_largest(s, PEER_TOPK))
        v1 = jnp.zeros((PEER_TOPK, t), F32)
        for a in range(PEER_TOPK):
            v1 = jnp.where(row16 == a, top[0][a], v1)
        v2 = top[1]
        cand = [v1 + v2[0]]
        for b in range(1, PEER_TOPK):
            cand.append(jnp.where(row8 < _cand_rows(b), v1[:8] + v2[b], NEG_INF))
        cand = jnp.concatenate(cand, axis=0)
        tau = _kth_largest(cand, PEER_TOPK)[-1]
        cmax = top[0][0] + v2[0]
        z = jnp.sum(jnp.where(cand >= tau, jnp.exp(cand - cmax), 0.0), axis=0, keepdims=True)
        tau_ref[hd:hd + 1, :] = tau
        e_ref[2 * hd] = jnp.exp(sc[0] - top[0][0]) / z
        e_ref[2 * hd + 1] = jnp.exp(sc[1] - v2[0])


def _peer_route(x, shift, scale, gffn, wqt, subk, tile):
    nb, s, _ = x.shape
    r = shift.shape[1]
    ns = s // tile
    n = nb * s
    rt = tile if r == s else 1
    tok = pl.BlockSpec((1, tile, D_MODEL), lambda b, i: (b, i, 0))
    mod = pl.BlockSpec((1, rt, D_MODEL), (lambda b, i: (b, i, 0)) if r == s else (lambda b, i: (b, 0, 0)))
    full = lambda a: pl.BlockSpec(a.shape, lambda b, i: (0,) * a.ndim)
    n_hp = 2 * PEER_HEADS
    return pl.pallas_call(
        _peer_route_kernel,
        grid=(nb, ns),
        in_specs=[tok, mod, mod, full(gffn), full(wqt), full(subk)],
        out_specs=[pl.BlockSpec((D_MODEL, tile), lambda b, i: (0, b * ns + i)),
                   pl.BlockSpec((n_hp, PEER_N_KEYS, tile), lambda b, i: (0, 0, b * ns + i)),
                   pl.BlockSpec((n_hp, PEER_N_KEYS, tile), lambda b, i: (0, 0, b * ns + i)),
                   pl.BlockSpec((PEER_HEADS, tile), lambda b, i: (0, b * ns + i))],
        out_shape=[jax.ShapeDtypeStruct((D_MODEL, n), BF16),
                   jax.ShapeDtypeStruct((n_hp, PEER_N_KEYS, n), F32),
                   jax.ShapeDtypeStruct((n_hp, PEER_N_KEYS, n), F32),
                   jax.ShapeDtypeStruct((PEER_HEADS, n), F32)],
        compiler_params=_cparams(2),
        name="peer_route",
    )(x, shift, scale, gffn, wqt, subk)


def _peer_sweep_kernel(ht_ref, s_ref, e_ref, tau_ref, u_ref, vt_ref, x_ref, gate_ref, o_ref,
                       a_s, ga_s, yt_s):
    j = pl.program_id(2)
    t = ht_ref.shape[1]

    @pl.when(j == 0)
    def _():
        yt_s[...] = jnp.zeros(yt_s.shape, F32)

    a_s[...] = jnp.dot(u_ref[...], ht_ref[...], preferred_element_type=F32)

    def key_row(blk, carry):
        i1 = j * KEY_ROWS_PER_CHUNK + blk
        g = jnp.zeros((PEER_N_KEYS, t), F32)
        for hd in range(PEER_HEADS):
            s1 = s_ref[2 * hd, pl.ds(i1, 1), :]
            e1 = e_ref[2 * hd, pl.ds(i1, 1), :]
            keep = (s_ref[2 * hd + 1] + s1) >= tau_ref[hd:hd + 1, :]
            g = g + jnp.where(keep, e_ref[2 * hd + 1] * e1, 0.0)
        rows = pl.ds(pl.multiple_of(blk * PEER_N_KEYS, PEER_N_KEYS), PEER_N_KEYS)
        ga_s[rows, :] = (g * _gelu(a_s[rows, :])).astype(BF16)
        return carry

    lax.fori_loop(0, KEY_ROWS_PER_CHUNK, key_row, 0)
    yt_s[...] += jnp.dot(vt_ref[...], ga_s[...], preferred_element_type=F32)

    @pl.when(j == pl.num_programs(2) - 1)
    def _():
        o_ref[0] = x_ref[0] + gate_ref[0] * yt_s[...].T


def _peer_sweep(ht, sc, ex, tau, u, vt, x, gate, tile):
    nb, s, _ = x.shape
    r = gate.shape[1]
    ns = s // tile
    rt = tile if r == s else 1
    n_chunks = PEER_N_EXPERTS // EXPERT_CHUNK
    n_hp = 2 * PEER_HEADS
    tok = pl.BlockSpec((1, tile, D_MODEL), lambda b, i, j: (b, i, 0))
    mod = pl.BlockSpec((1, rt, D_MODEL),
                       (lambda b, i, j: (b, i, 0)) if r == s else (lambda b, i, j: (b, 0, 0)))
    return pl.pallas_call(
        _peer_sweep_kernel,
        grid=(nb, ns, n_chunks),
        in_specs=[pl.BlockSpec((D_MODEL, tile), lambda b, i, j: (0, b * ns + i)),
                  pl.BlockSpec((n_hp, PEER_N_KEYS, tile), lambda b, i, j: (0, 0, b * ns + i)),
                  pl.BlockSpec((n_hp, PEER_N_KEYS, tile), lambda b, i, j: (0, 0, b * ns + i)),
                  pl.BlockSpec((PEER_HEADS, tile), lambda b, i, j: (0, b * ns + i)),
                  pl.BlockSpec((EXPERT_CHUNK, D_MODEL), lambda b, i, j: (j, 0)),
                  pl.BlockSpec((D_MODEL, EXPERT_CHUNK), lambda b, i, j: (0, j)),
                  tok, mod],
        out_specs=tok,
        out_shape=jax.ShapeDtypeStruct(x.shape, F32),
        scratch_shapes=[pltpu.VMEM((EXPERT_CHUNK, tile), F32), pltpu.VMEM((EXPERT_CHUNK, tile), BF16),
                        pltpu.VMEM((D_MODEL, tile), F32)],
        compiler_params=_cparams(3),
        name="peer_sweep",
    )(ht, sc, ex, tau, u, vt, x, gate)


def _peer(x, shift, scale, gate, gffn, pw, tile):
    ht, sc, ex, tau = _peer_route(x, shift, scale, gffn, pw["wqt"], pw["subk"], tile)
    return _peer_sweep(ht, sc, ex, tau, pw["u"], pw["vt"], x, gate, tile)


def _rope_tables(pos):
    inv = ROPE_THETA ** (-jnp.arange(0, QK_ROPE_DIM, 2, dtype=F32) / QK_ROPE_DIM)
    ang = pos.astype(F32)[:, None] * inv[None, :]
    c, s, z = jnp.cos(ang), jnp.sin(ang), jnp.zeros_like(ang)
    return jnp.concatenate([c, z, c, z], -1), jnp.concatenate([-s, z, s, z], -1)


def _pad_rope_cols(w):
    z = jnp.zeros(w.shape[:-1] + (ROPE_HALF,), w.dtype)
    return jnp.concatenate([w[..., :ROPE_HALF], z, w[..., ROPE_HALF:], z], -1)


def _mla_weights(w_in, g_q_lat, g_kv_lat, w_uq, w_uk, w_uv, g_qn_nope, g_qn_rope, g_kn_nope,
                 g_kn_rope, w_out):
    lat_end = Q_LORA_RANK + KV_LORA_RANK
    w_in_p = jnp.concatenate([w_in[:, :lat_end], _pad_rope_cols(w_in[:, lat_end:])], -1)
    wq = w_uq.reshape(Q_LORA_RANK, N_HEADS, QK_NOPE_DIM + QK_ROPE_DIM)
    wq = jnp.concatenate([wq[..., :QK_NOPE_DIM], _pad_rope_cols(wq[..., QK_NOPE_DIM:])], -1)
    row = lambda g: g.reshape(1, -1)
    return {
        "w_in": w_in_p.astype(BF16),
        "g_q_lat": row(g_q_lat), "g_kv_lat": row(g_kv_lat), "g_kr": row(_pad_rope_cols(g_kn_rope)),
        "w_uq": wq.reshape(Q_LORA_RANK, N_HEADS * HEAD_PAD).astype(BF16),
        "g_qn": row(g_qn_nope), "g_qr": row(_pad_rope_cols(g_qn_rope)),
        "w_uk": w_uk.reshape(KV_LORA_RANK, N_HEADS * QK_NOPE_DIM).astype(BF16),
        "g_kn": row(g_kn_nope),
        "w_ukt": jnp.transpose(w_uk, (1, 2, 0)).astype(BF16),
        "w_uv": jnp.transpose(w_uv, (1, 0, 2)).astype(BF16),
        "w_out": w_out.astype(BF16),
    }


def _peer_weights(w_q, subkeys, u, v):
    return {
        "wqt": w_q.T.astype(BF16),
        "subk": subkeys.reshape(2 * PEER_HEADS, PEER_N_KEYS, PEER_HALF).astype(BF16),
        "u": u.astype(BF16),
        "vt": v.T.astype(BF16),
    }


PROMPT_TILE = 256
SAMPLE_TILE = 128


def kernel(x_prompt, x_sample, c_prompt, c_sample, cache_mla, state_pool, page_table, norm_mix, norm_ffn, w_ada, b_ada, w_mla_in, g_q_lat, g_kv_lat, w_uq, w_uk, w_uv, g_qn_nope, g_qn_rope, g_kn_nope, g_kn_rope, w_mla_out, w_pool, pool_scale, w_peer_q, peer_subkeys, peer_u, peer_v):
    nbp, seq, _ = x_prompt.shape
    nbs = x_sample.shape[0]
    depth = w_ada.shape[0]
    xp = x_prompt
    xs = x_sample.reshape(1, nbs, D_MODEL)

    mods = _ada_params(jnp.concatenate([c_prompt, c_sample], 0), w_ada, b_ada)
    mods = mods.reshape(depth, nbp + nbs, 6, D_MODEL)
    cs_p, sn_p = _rope_tables(jnp.arange(seq))
    cs_s, sn_s = _rope_tables(jnp.full((nbs,), PAST_LEN))
    hsel = (jnp.arange(N_HEADS)[:, None] == jnp.arange(N_HEADS * QK_NOPE_DIM)[None, :] // QK_NOPE_DIM
            ).astype(BF16)

    rows_p, rows_s, pst_p, pst_s = [], [], [], []
    for l in range(depth):
        mp = [mods[l, :nbp, k].reshape(nbp, 1, D_MODEL) for k in range(6)]
        ms = [mods[l, nbp:, k].reshape(1, nbs, D_MODEL) for k in range(6)]
        gmix = norm_mix[l].reshape(1, D_MODEL)
        gffn = norm_ffn[l].reshape(1, D_MODEL)
        i = l // 2
        if l % 2 == 0:
            mw = _mla_weights(w_mla_in[i], g_q_lat[i], g_kv_lat[i], w_uq[i], w_uk[i], w_uv[i],
                              g_qn_nope[i], g_qn_rope[i], g_kn_nope[i], g_kn_rope[i], w_mla_out[i])
            r_p, q, k, v = _mla_project(xp, mp[0], mp[1], gmix, cs_p, sn_p, mw, PROMPT_TILE, False)
            xp = _prompt_attention(q, k, v, xp, mp[2], mw["w_uv"], mw["w_out"], PROMPT_TILE)
            r_s, q, k, v, qa, sself = _mla_project(xs, ms[0], ms[1], gmix, cs_s, sn_s, mw,
                                                   SAMPLE_TILE, True)
            q = q.reshape(nbs, N_HEADS, HEAD_PAD)[:, :, QK_NOPE_DIM:]
            qr = jnp.concatenate([q[..., :ROPE_HALF], q[..., 2 * ROPE_HALF:3 * ROPE_HALF]], -1)
            ctx = _sample_attention(
                cache_mla[i], page_table, qa.reshape(nbs, N_HEADS, KV_LORA_RANK), qr,
                sself.reshape(nbs, 128)[:, :N_HEADS, None], r_s.reshape(nbs, 1, CACHE_ROW)[..., :KV_LORA_RANK],
                hsel, mw["w_uk"])
            xs = _sample_out(ctx.reshape(nbs, N_HEADS * KV_LORA_RANK), xs[0], ms[2][0],
                             mw["w_uv"], mw["w_out"]).reshape(1, nbs, D_MODEL)
            rows_p.append(r_p)
            rows_s.append(r_s.reshape(nbs, 1, CACHE_ROW))
        else:
            wp = w_pool[i].astype(BF16)
            psc = pool_scale[i].reshape(1, D_MODEL)
            xp, st_p = _pool_prompt(xp, mp[0], mp[1], mp[2], gmix, wp, psc, PROMPT_TILE)
            x2, st_s = _pool_sample(xs[0], ms[0][0], ms[1][0], ms[2][0], gmix, wp, psc,
                                    jnp.transpose(state_pool[i], (1, 0, 2)))
            xs = x2.reshape(1, nbs, D_MODEL)
            pst_p.append(st_p)
            pst_s.append(jnp.transpose(st_s, (1, 0, 2)))
        pw = _peer_weights(w_peer_q[l], peer_subkeys[l], peer_u[l], peer_v[l])
        xp = _peer(xp, mp[3], mp[4], mp[5], gffn, pw, PROMPT_TILE)
        xs = _peer(xs, ms[3], ms[4], ms[5], gffn, pw, SAMPLE_TILE)
    return (xp, xs.reshape(nbs, 1, D_MODEL), jnp.stack(rows_p), jnp.stack(rows_s),
            jnp.stack(pst_p), jnp.stack(pst_s))
```

```python
import functools

import jax
import jax.numpy as jnp
from jax import lax
from jax.experimental import pallas as pl
from jax.experimental.pallas import tpu as pltpu

F32 = jnp.float32
BF16 = jnp.bfloat16

D_MODEL = 1024
EPS = 1e-6
PAST_LEN = 8192
PAGE_SIZE = 128
N_HEADS = 8
QK_NOPE_DIM = 128
QK_ROPE_DIM = 64
ROPE_HALF = QK_ROPE_DIM // 2
ROPE_PAD = 128
HEAD_PAD = QK_NOPE_DIM + ROPE_PAD
V_HEAD_DIM = 128
Q_LORA_RANK = 384
KV_LORA_RANK = 256
CACHE_ROW = KV_LORA_RANK + QK_ROPE_DIM
ROPE_THETA = 10000.0
ATTN_SCALE = (QK_NOPE_DIM + QK_ROPE_DIM) ** -0.5
POOL_WINDOWS = (2, 4, 8, 16)
POOL_GROUP_DIM = D_MODEL // len(POOL_WINDOWS)
POOL_STATE_LEN = max(POOL_WINDOWS) - 1
POOL_HALO = 16
PEER_HEADS = 8
PEER_N_KEYS = 128
PEER_N_EXPERTS = PEER_N_KEYS * PEER_N_KEYS
PEER_TOPK = 16
PEER_HALF = 128
EXPERT_CHUNK = 2048
KEY_ROWS_PER_CHUNK = EXPERT_CHUNK // PEER_N_KEYS

VMEM_LIMIT = 56 * 1024 * 1024

NEG_INF = float("-inf")


def _cparams(n_grid):
    return pltpu.CompilerParams(
        dimension_semantics=("arbitrary",) * n_grid, vmem_limit_bytes=VMEM_LIMIT)


def _rms(x, g, n=None):
    n = x.shape[-1] if n is None else n
    ms = jnp.sum(x * x, axis=-1, keepdims=True) * (1.0 / n)
    return x * lax.rsqrt(ms + EPS) * g


def _gelu(x):
    return 0.5 * x * (1.0 + lax.erf(x * (2.0 ** -0.5)))


def _bdot(a, b):
    return jnp.dot(a.astype(BF16), b.astype(BF16), preferred_element_type=F32)


def _bdot_nt(a, b):
    return lax.dot_general(a.astype(BF16), b.astype(BF16), (((1,), (1,)), ((), ())),
                           preferred_element_type=F32)


def _mods_kernel(c_ref, w_ref, b_ref, o_ref):
    o_ref[0] = _bdot(jax.nn.silu(c_ref[...]), w_ref[0]) + b_ref[0]


def _ada_params(c_all, w_ada, b_ada):
    n_layers = w_ada.shape[0]
    n_rows = c_all.shape[0]
    n_col = w_ada.shape[2] // D_MODEL
    return pl.pallas_call(
        _mods_kernel,
        grid=(n_layers, n_col),
        in_specs=[
            pl.BlockSpec((n_rows, D_MODEL), lambda l, k: (0, 0)),
            pl.BlockSpec((1, D_MODEL, D_MODEL), lambda l, k: (l, 0, k)),
            pl.BlockSpec((1, 1, D_MODEL), lambda l, k: (l, 0, k)),
        ],
        out_specs=pl.BlockSpec((1, n_rows, D_MODEL), lambda l, k: (l, 0, k)),
        out_shape=jax.ShapeDtypeStruct((n_layers, n_rows, n_col * D_MODEL), F32),
        compiler_params=_cparams(2),
        name="ada_params",
    )(c_all, w_ada, b_ada.reshape(n_layers, 1, -1))


def _rope_pad(x, g, cs, sn):
    y = _rms(x, g, QK_ROPE_DIM)
    return y * cs + pltpu.roll(y, ROPE_PAD // 2, axis=1) * sn


def _mla_proj_kernel(absorb, x_ref, shift_ref, scale_ref, gmix_ref, cs_ref, sn_ref, w_in_ref,
                     g_q_lat_ref, g_kv_lat_ref, g_kr_ref, w_uq_ref, g_qn_ref, g_qr_ref,
                     w_uk_ref, g_kn_ref, *rest):
    if absorb:
        w_ukt_ref, rows_ref, q_ref, k_ref, v_ref, qa_ref, sself_ref = rest
    else:
        rows_ref, q_ref, k_ref, v_ref = rest
    cs = cs_ref[...]
    sn = sn_ref[...]
    h = _rms(x_ref[0], gmix_ref[...]) * (1.0 + scale_ref[0]) + shift_ref[0]
    z = _bdot(h, w_in_ref[...])
    cq = _rms(z[:, :Q_LORA_RANK], g_q_lat_ref[...])
    lat = _rms(z[:, Q_LORA_RANK:Q_LORA_RANK + KV_LORA_RANK], g_kv_lat_ref[...])
    kr = _rope_pad(z[:, Q_LORA_RANK + KV_LORA_RANK:], g_kr_ref[...], cs, sn)
    rows_ref[0, :, :KV_LORA_RANK] = lat
    rows_ref[0, :, KV_LORA_RANK:] = (kr + pltpu.roll(kr, ROPE_PAD - ROPE_HALF, axis=1))[:, :QK_ROPE_DIM]
    v_ref[0] = lat.astype(BF16)
    q = _bdot(cq, w_uq_ref[...])
    kn_all = _bdot(lat, w_uk_ref[...])
    kr_b = kr.astype(BF16)
    if absorb:
        lane = lax.broadcasted_iota(jnp.int32, (x_ref.shape[1], 128), 1)
        sself = jnp.zeros((x_ref.shape[1], 128), F32)
    for hd in range(N_HEADS):
        o = hd * HEAD_PAD
        qn = _rms(q[:, o:o + QK_NOPE_DIM], g_qn_ref[...]) * ATTN_SCALE
        qr = _rope_pad(q[:, o + QK_NOPE_DIM:o + HEAD_PAD], g_qr_ref[...], cs, sn) * ATTN_SCALE
        kn = _rms(kn_all[:, hd * QK_NOPE_DIM:(hd + 1) * QK_NOPE_DIM], g_kn_ref[...])
        qn_b, qr_b, kn_b = qn.astype(BF16), qr.astype(BF16), kn.astype(BF16)
        q_ref[0, :, o:o + QK_NOPE_DIM] = qn_b
        q_ref[0, :, o + QK_NOPE_DIM:o + HEAD_PAD] = qr_b
        k_ref[0, :, o:o + QK_NOPE_DIM] = kn_b
        k_ref[0, :, o + QK_NOPE_DIM:o + HEAD_PAD] = kr_b
        if absorb:
            s = (jnp.sum(qn_b.astype(F32) * kn_b.astype(F32), axis=-1, keepdims=True)
                 + jnp.sum(qr_b.astype(F32) * kr_b.astype(F32), axis=-1, keepdims=True))
            sself = jnp.where(lane == hd, s, sself)
            qa = _bdot(qn * g_kn_ref[...], w_ukt_ref[hd])
            qa_ref[0, :, hd * KV_LORA_RANK:(hd + 1) * KV_LORA_RANK] = qa.astype(BF16)
    if absorb:
        sself_ref[0] = sself


def _mla_project(x, shift, scale, gmix, cs, sn, mw, tile, absorb):
    nb, s, _ = x.shape
    r = shift.shape[1]
    ns = s // tile
    rt = tile if r == s else 1
    tok = lambda bs: pl.BlockSpec(bs, lambda b, i: (b, i, 0))
    mod = pl.BlockSpec((1, rt, D_MODEL), (lambda b, i: (b, i, 0)) if r == s else (lambda b, i: (b, 0, 0)))
    full = lambda a: pl.BlockSpec(a.shape, lambda b, i: (0,) * a.ndim)
    weights = [mw["w_in"], mw["g_q_lat"], mw["g_kv_lat"], mw["g_kr"], mw["w_uq"], mw["g_qn"],
               mw["g_qr"], mw["w_uk"], mw["g_kn"]]
    if absorb:
        weights.append(mw["w_ukt"])
    out_shape = [jax.ShapeDtypeStruct((nb, s, CACHE_ROW), F32),
                 jax.ShapeDtypeStruct((nb, s, N_HEADS * HEAD_PAD), BF16),
                 jax.ShapeDtypeStruct((nb, s, N_HEADS * HEAD_PAD), BF16),
                 jax.ShapeDtypeStruct((nb, s, KV_LORA_RANK), BF16)]
    out_specs = [tok((1, tile, CACHE_ROW)), tok((1, tile, N_HEADS * HEAD_PAD)),
                 tok((1, tile, N_HEADS * HEAD_PAD)), tok((1, tile, KV_LORA_RANK))]
    if absorb:
        out_shape += [jax.ShapeDtypeStruct((nb, s, N_HEADS * KV_LORA_RANK), BF16),
                      jax.ShapeDtypeStruct((nb, s, 128), F32)]
        out_specs += [tok((1, tile, N_HEADS * KV_LORA_RANK)), tok((1, tile, 128))]
    return pl.pallas_call(
        functools.partial(_mla_proj_kernel, absorb),
        grid=(nb, ns),
        in_specs=[tok((1, tile, D_MODEL)), mod, mod, full(gmix),
                  pl.BlockSpec((tile, ROPE_PAD), lambda b, i: (i, 0)),
                  pl.BlockSpec((tile, ROPE_PAD), lambda b, i: (i, 0))] + [full(w) for w in weights],
        out_specs=out_specs,
        out_shape=out_shape,
        compiler_params=_cparams(2),
        name="mla_project_sample" if absorb else "mla_project",
    )(x, shift, scale, gmix, cs, sn, *weights)


def _attn_out(ctx_of_head, x, gate, w_uv_ref, w_out_ref):
    o = jnp.concatenate([_bdot(ctx_of_head(hd), w_uv_ref[hd]) for hd in range(N_HEADS)], axis=-1)
    return x + gate * _bdot(o, w_out_ref[...])


def _flash_kernel(q_ref, k_ref, v_ref, x_ref, gate_ref, w_uv_ref, w_out_ref, o_ref,
                  m_s, l_s, acc_s):
    qi = pl.program_id(1)
    ki = pl.program_id(2)
    tq = q_ref.shape[1]

    @pl.when(ki == 0)
    def _():
        m_s[...] = jnp.full(m_s.shape, NEG_INF, F32)
        l_s[...] = jnp.zeros(l_s.shape, F32)
        acc_s[...] = jnp.zeros(acc_s.shape, F32)

    def step(masked):
        v = v_ref[0]
        if masked:
            row = lax.broadcasted_iota(jnp.int32, (tq, tq), 0)
            col = lax.broadcasted_iota(jnp.int32, (tq, tq), 1)
            keep = col <= row
        for hd in range(N_HEADS):
            o = hd * HEAD_PAD
            s = _bdot_nt(q_ref[0, :, o:o + HEAD_PAD], k_ref[0, :, o:o + HEAD_PAD])
            if masked:
                s = jnp.where(keep, s, NEG_INF)
            m_prev = m_s[hd]
            m_new = jnp.maximum(m_prev, jnp.max(s, axis=-1, keepdims=True))
            alpha = jnp.exp(m_prev - m_new)
            p = jnp.exp(s - m_new)
            l_s[hd] = alpha * l_s[hd] + jnp.sum(p, axis=-1, keepdims=True)
            acc_s[hd] = alpha * acc_s[hd] + _bdot(p, v)
            m_s[hd] = m_new

    @pl.when(ki < qi)
    def _():
        step(False)

    @pl.when(ki == qi)
    def _():
        step(True)
        o_ref[0] = _attn_out(lambda hd: acc_s[hd] / l_s[hd], x_ref[0], gate_ref[0],
                             w_uv_ref, w_out_ref)


def _prompt_attention(q, k, v, x, gate, w_uv, w_out, tile):
    nb, s, _ = x.shape
    nt = s // tile
    full = lambda a: pl.BlockSpec(a.shape, lambda b, i, j: (0,) * a.ndim)
    return pl.pallas_call(
        _flash_kernel,
        grid=(nb, nt, nt),
        in_specs=[
            pl.BlockSpec((1, tile, N_HEADS * HEAD_PAD), lambda b, i, j: (b, i, 0)),
            pl.BlockSpec((1, tile, N_HEADS * HEAD_PAD), lambda b, i, j: (b, jnp.minimum(i, j), 0)),
            pl.BlockSpec((1, tile, KV_LORA_RANK), lambda b, i, j: (b, jnp.minimum(i, j), 0)),
            pl.BlockSpec((1, tile, D_MODEL), lambda b, i, j: (b, i, 0)),
            pl.BlockSpec((1, 1, D_MODEL), lambda b, i, j: (b, 0, 0)),
            full(w_uv), full(w_out),
        ],
        out_specs=pl.BlockSpec((1, tile, D_MODEL), lambda b, i, j: (b, i, 0)),
        out_shape=jax.ShapeDtypeStruct(x.shape, F32),
        scratch_shapes=[pltpu.VMEM((N_HEADS, tile, 1), F32), pltpu.VMEM((N_HEADS, tile, 1), F32),
                        pltpu.VMEM((N_HEADS, tile, KV_LORA_RANK), F32)],
        compiler_params=_cparams(3),
        name="prompt_attention",
    )(q, k, v, x, gate, w_uv, w_out)


PAGES_PER_STEP = 16
PAGES_PER_DOT = 4


def _sample_attn_kernel(pt_ref, qa_ref, qr_ref, sself_ref, lat_new_ref, hsel_ref, w_uk_ref, *rest):
    del pt_ref
    pages = rest[:PAGES_PER_STEP]
    ctx_ref, m_s, l_s, acc_s = rest[PAGES_PER_STEP:]
    step = pl.program_id(1)

    @pl.when(step == 0)
    def _():
        m_s[...] = sself_ref[0]
        l_s[...] = jnp.ones(l_s.shape, F32)
        acc_s[...] = jnp.broadcast_to(lat_new_ref[0], acc_s.shape)

    qa = qa_ref[0]
    qr = qr_ref[0]
    for c in range(PAGES_PER_STEP // PAGES_PER_DOT):
        rows = jnp.concatenate([pages[c * PAGES_PER_DOT + p][...] for p in range(PAGES_PER_DOT)], axis=0)
        lat = rows[:, :KV_LORA_RANK].astype(BF16)
        kr = rows[:, KV_LORA_RANK:].astype(BF16)
        y = _bdot(lat, w_uk_ref[...])
        ssq = _bdot_nt(hsel_ref[...], y * y)
        rinv = lax.rsqrt(ssq * (1.0 / QK_NOPE_DIM) + EPS)
        s = _bdot_nt(qa, lat) * rinv + _bdot_nt(qr, kr)
        m_prev = m_s[...]
        m_new = jnp.maximum(m_prev, jnp.max(s, axis=-1, keepdims=True))
        alpha = jnp.exp(m_prev - m_new)
        p = jnp.exp(s - m_new)
        l_s[...] = alpha * l_s[...] + jnp.sum(p, axis=-1, keepdims=True)
        acc_s[...] = alpha * acc_s[...] + _bdot(p, lat)
        m_s[...] = m_new

    @pl.when(step == pl.num_programs(1) - 1)
    def _():
        ctx_ref[0] = acc_s[...] / l_s[...]


def _sample_attention(cache, layer, page_table, qa, qr, sself, lat_new, hsel, w_uk):
    nb, n_pages = page_table.shape
    n_steps = n_pages // PAGES_PER_STEP
    per_b = lambda a: pl.BlockSpec((1,) + a.shape[1:], lambda b, s, pt: (b,) + (0,) * (a.ndim - 1))
    full = lambda a: pl.BlockSpec(a.shape, lambda b, s, pt: (0,) * a.ndim)

    def page_spec(j):
        return pl.BlockSpec((None, None, PAGE_SIZE, CACHE_ROW),
                            lambda b, s, pt: (layer, pt[b * n_pages + s * PAGES_PER_STEP + j], 0, 0))

    grid_spec = pltpu.PrefetchScalarGridSpec(
        num_scalar_prefetch=1,
        grid=(nb, n_steps),
        in_specs=[per_b(qa), per_b(qr), per_b(sself), per_b(lat_new), full(hsel), full(w_uk)]
        + [page_spec(j) for j in range(PAGES_PER_STEP)],
        out_specs=pl.BlockSpec((1, N_HEADS, KV_LORA_RANK), lambda b, s, pt: (b, 0, 0)),
        scratch_shapes=[pltpu.VMEM((N_HEADS, 1), F32), pltpu.VMEM((N_HEADS, 1), F32),
                        pltpu.VMEM((N_HEADS, KV_LORA_RANK), F32)],
    )
    return pl.pallas_call(
        _sample_attn_kernel,
        grid_spec=grid_spec,
        out_shape=jax.ShapeDtypeStruct((nb, N_HEADS, KV_LORA_RANK), F32),
        compiler_params=_cparams(2),
        name="sample_attention",
    )(page_table.reshape(-1), qa, qr, sself, lat_new, hsel, w_uk, *([cache] * PAGES_PER_STEP))


def _sample_out_kernel(ctx_ref, x_ref, gate_ref, w_uv_ref, w_out_ref, o_ref):
    o_ref[...] = _attn_out(lambda hd: ctx_ref[:, hd * KV_LORA_RANK:(hd + 1) * KV_LORA_RANK],
                           x_ref[...], gate_ref[...], w_uv_ref, w_out_ref)


def _sample_out(ctx, x, gate, w_uv, w_out):
    return pl.pallas_call(
        _sample_out_kernel,
        out_shape=jax.ShapeDtypeStruct(x.shape, F32),
        compiler_params=pltpu.CompilerParams(vmem_limit_bytes=VMEM_LIMIT),
        name="sample_attention_out",
    )(ctx, x, gate, w_uv, w_out)


def _pool_mix(window_sum, h, cnt, w_pool_ref, pscale):
    parts = []
    for gi, w in enumerate(POOL_WINDOWS):
        ch = slice(gi * POOL_GROUP_DIM, (gi + 1) * POOL_GROUP_DIM)
        acc = h[:, ch]
        for kk in range(1, w):
            acc = acc + window_sum(kk, ch)
        d = acc / cnt[gi] - h[:, ch]
        parts.append(_bdot(d, w_pool_ref[gi]))
    return jnp.concatenate(parts, axis=-1) * pscale


def _pool_prompt_kernel(x_ref, shift_ref, scale_ref, gate_ref, gmix_ref, w_pool_ref, pscale_ref,
                        o_ref, st_ref, ext_s):
    si = pl.program_id(1)
    t = x_ref.shape[1]

    @pl.when(si == 0)
    def _():
        ext_s[0:POOL_HALO] = jnp.zeros((POOL_HALO, D_MODEL), F32)

    @pl.when(si > 0)
    def _():
        ext_s[0:POOL_HALO] = ext_s[t:t + POOL_HALO]

    x = x_ref[0]
    h = _rms(x, gmix_ref[...]) * (1.0 + scale_ref[0]) + shift_ref[0]
    ext_s[POOL_HALO:POOL_HALO + t] = h
    pos = si * t + lax.broadcasted_iota(jnp.int32, (t, 1), 0)
    cnt = [jnp.minimum(pos + 1, w).astype(F32) for w in POOL_WINDOWS]
    y = _pool_mix(lambda kk, ch: ext_s[POOL_HALO - kk:POOL_HALO - kk + t, ch], h, cnt,
                  w_pool_ref, pscale_ref[...])
    o_ref[0] = x + gate_ref[0] * y

    @pl.when(si == pl.num_programs(1) - 1)
    def _():
        st_ref[0] = ext_s[POOL_HALO + t - POOL_STATE_LEN:POOL_HALO + t]


def _pool_prompt(x, shift, scale, gate, gmix, w_pool, pscale, tile):
    nb, s, _ = x.shape
    tok = pl.BlockSpec((1, tile, D_MODEL), lambda b, i: (b, i, 0))
    mod = pl.BlockSpec((1, 1, D_MODEL), lambda b, i: (b, 0, 0))
    full = lambda a: pl.BlockSpec(a.shape, lambda b, i: (0,) * a.ndim)
    return pl.pallas_call(
        _pool_prompt_kernel,
        grid=(nb, s // tile),
        in_specs=[tok, mod, mod, mod, full(gmix), full(w_pool), full(pscale)],
        out_specs=[tok, pl.BlockSpec((1, POOL_STATE_LEN, D_MODEL), lambda b, i: (b, 0, 0))],
        out_shape=[jax.ShapeDtypeStruct(x.shape, F32),
                   jax.ShapeDtypeStruct((nb, POOL_STATE_LEN, D_MODEL), F32)],
        scratch_shapes=[pltpu.VMEM((POOL_HALO + tile, D_MODEL), F32)],
        compiler_params=_cparams(2),
        name="pool_prompt",
    )(x, shift, scale, gate, gmix, w_pool, pscale)


def _pool_sample_kernel(x_ref, shift_ref, scale_ref, gate_ref, gmix_ref, w_pool_ref, pscale_ref,
                        state_ref, o_ref, st_ref):
    x = x_ref[...]
    h = _rms(x, gmix_ref[...]) * (1.0 + scale_ref[...]) + shift_ref[...]
    cnt = [float(w) for w in POOL_WINDOWS]
    y = _pool_mix(lambda kk, ch: state_ref[POOL_STATE_LEN - kk, :, ch], h, cnt,
                  w_pool_ref, pscale_ref[...])
    o_ref[...] = x + gate_ref[...] * y
    for r in range(POOL_STATE_LEN - 1):
        st_ref[r] = state_ref[r + 1]
    st_ref[POOL_STATE_LEN - 1] = h


def _pool_sample(x, shift, scale, gate, gmix, w_pool, pscale, state_t):
    return pl.pallas_call(
        _pool_sample_kernel,
        out_shape=[jax.ShapeDtypeStruct(x.shape, F32), jax.ShapeDtypeStruct(state_t.shape, F32)],
        compiler_params=pltpu.CompilerParams(vmem_limit_bytes=VMEM_LIMIT),
        name="pool_sample",
    )(x, shift, scale, gate, gmix, w_pool, pscale, state_t)


def _cand_rows(b):
    return PEER_TOPK // (b + 1)


NOT_TOP = 64.0
LANES = 128


def _extract_top(s, want_rank):
    work = s
    rank = jnp.full(s.shape, NOT_TOP, F32) if want_rank else None
    vals = []
    for r in range(PEER_TOPK):
        m = jnp.max(work, axis=0, keepdims=True)
        vals.append(m)
        hit = work == m
        if want_rank:
            rank = jnp.where(hit, float(r), rank)
        work = jnp.where(hit, NEG_INF, work)
    return vals, rank


def _dup_bf16(x):
    hi = pltpu.bitcast(x.astype(BF16).astype(F32), jnp.uint32)
    return hi | (hi >> 16)


def _peer_route_kernel(x_ref, shift_ref, scale_ref, gffn_ref, wqt_ref, subk_ref,
                       ht_ref, r2_ref, e2_ref, n1_ref, e1_ref):
    t = x_ref.shape[1]
    h = _rms(x_ref[0], gffn_ref[...]) * (1.0 + scale_ref[0]) + shift_ref[0]
    ht = h.T.astype(BF16)
    ht_ref[...] = ht
    qt = jnp.dot(wqt_ref[...], ht, preferred_element_type=F32).astype(BF16)
    row16 = lax.broadcasted_iota(jnp.int32, (PEER_TOPK, LANES), 0)
    row8 = lax.broadcasted_iota(jnp.int32, (8, LANES), 0)
    for hd in range(PEER_HEADS):
        sc = [jnp.dot(subk_ref[2 * hd + p], qt[(2 * hd + p) * PEER_HALF:(2 * hd + p + 1) * PEER_HALF],
                      preferred_element_type=F32) for p in range(2)]
        for lt in range(t // LANES):
            ln = slice(lt * LANES, (lt + 1) * LANES)
            s1, s2 = sc[0][:, ln], sc[1][:, ln]
            top1, _ = _extract_top(s1, False)
            top2, rank2 = _extract_top(s2, True)
            v1 = jnp.zeros((PEER_TOPK, LANES), F32)
            for a in range(PEER_TOPK):
                v1 = jnp.where(row16 == a, top1[a], v1)
            cand = [v1 + top2[0]]
            for b in range(1, PEER_TOPK):
                cand.append(jnp.where(row8 < _cand_rows(b), v1[:8] + top2[b], NEG_INF))
            cand = jnp.concatenate(cand, axis=0)
            tau = _extract_top(cand, False)[0][-1]
            cmax = top1[0] + top2[0]
            z = jnp.sum(jnp.where(cand >= tau, jnp.exp(cand - cmax), 0.0), axis=0, keepdims=True)
            n1 = jnp.zeros((PEER_N_KEYS, LANES), F32)
            for b in range(PEER_TOPK):
                theta = jnp.min(jnp.where(v1 + top2[b] >= tau, v1, -NEG_INF), axis=0, keepdims=True)
                n1 = n1 + jnp.where(s1 >= theta, 1.0, 0.0)
            r2_ref[hd, :, ln] = rank2.astype(BF16)
            e2_ref[hd, :, ln] = jnp.exp(s2 - top2[0]).astype(BF16)
            n1_ref[hd, :, ln] = _dup_bf16(n1)
            e1_ref[hd, :, ln] = _dup_bf16(jnp.exp(s1 - top1[0]) * (0.5 / z))


def _peer_route(x, shift, scale, gffn, wqt, subk, tile):
    nb, s, _ = x.shape
    r = shift.shape[1]
    ns = s // tile
    n = nb * s
    rt = tile if r == s else 1
    tok = pl.BlockSpec((1, tile, D_MODEL), lambda b, i: (b, i, 0))
    mod = pl.BlockSpec((1, rt, D_MODEL), (lambda b, i: (b, i, 0)) if r == s else (lambda b, i: (b, 0, 0)))
    full = lambda a: pl.BlockSpec(a.shape, lambda b, i: (0,) * a.ndim)
    keyed = pl.BlockSpec((PEER_HEADS, PEER_N_KEYS, tile), lambda b, i: (0, 0, b * ns + i))
    keyed_shape = lambda dt: jax.ShapeDtypeStruct((PEER_HEADS, PEER_N_KEYS, n), dt)
    return pl.pallas_call(
        _peer_route_kernel,
        grid=(nb, ns),
        in_specs=[tok, mod, mod, full(gffn), full(wqt), full(subk)],
        out_specs=[pl.BlockSpec((D_MODEL, tile), lambda b, i: (0, b * ns + i)),
                   keyed, keyed, keyed, keyed],
        out_shape=[jax.ShapeDtypeStruct((D_MODEL, n), BF16), keyed_shape(BF16), keyed_shape(BF16),
                   keyed_shape(jnp.uint32), keyed_shape(jnp.uint32)],
        compiler_params=_cparams(2),
        name="peer_route",
    )(x, shift, scale, gffn, wqt, subk)


N_CHUNKS = PEER_N_EXPERTS // EXPERT_CHUNK
PIPE_DEPTH = 2
GATE_ROWS = 32


def _peer_sweep_kernel(ht_ref, r2_ref, e2_ref, n1_ref, e1_ref, u_ref, vt_ref, x_ref, gate_ref, o_ref,
                       a0_s, a1_s, g0_s, g1_s, yt_s):
    s = pl.program_id(0)
    t = ht_ref.shape[1]
    w_gate = jnp.maximum(s - 1, 0)
    w_out = jnp.maximum(s - PIPE_DEPTH, 0)

    @pl.when(s == 0)
    def _():
        for ref in (a0_s, a1_s, g0_s, g1_s):
            ref[...] = jnp.zeros(ref.shape, ref.dtype)

    @pl.when(w_out % N_CHUNKS == 0)
    def _():
        yt_s[...] = jnp.zeros(yt_s.shape, F32)

    key_row0 = (w_gate % N_CHUNKS) * KEY_ROWS_PER_CHUNK

    def stages(a_new, a_cur, ga_new, ga_cur):
        a_new[...] = jnp.dot(u_ref[...], ht_ref[...], preferred_element_type=F32)
        zero = jnp.zeros((GATE_ROWS, t), BF16)
        for blk in range(KEY_ROWS_PER_CHUNK):
            spread = lambda ref, hd: pltpu.bitcast(
                jnp.broadcast_to(ref[hd, pl.ds(key_row0 + blk, 1), :], (GATE_ROWS // 2, t)), BF16)
            n1 = [spread(n1_ref, hd) for hd in range(PEER_HEADS)]
            e1 = [spread(e1_ref, hd) for hd in range(PEER_HEADS)]
            for sg in range(PEER_N_KEYS // GATE_ROWS):
                keys = slice(sg * GATE_ROWS, (sg + 1) * GATE_ROWS)
                rows = slice(blk * PEER_N_KEYS + sg * GATE_ROWS, blk * PEER_N_KEYS + (sg + 1) * GATE_ROWS)
                g = zero
                for hd in range(PEER_HEADS):
                    keep = r2_ref[hd, keys, :] < n1[hd]
                    g = g + jnp.where(keep, e2_ref[hd, keys, :] * e1[hd], zero)
                a = a_cur[rows, :]
                ga_new[rows, :] = (a * (1.0 + lax.erf(a * (2.0 ** -0.5)))).astype(BF16) * g
        yt_s[...] += jnp.dot(vt_ref[...], ga_cur[...], preferred_element_type=F32)

    @pl.when(s % 2 == 0)
    def _():
        stages(a0_s, a1_s, g1_s, g0_s)

    @pl.when(s % 2 == 1)
    def _():
        stages(a1_s, a0_s, g0_s, g1_s)

    @pl.when(jnp.logical_and(s >= PIPE_DEPTH, w_out % N_CHUNKS == N_CHUNKS - 1))
    def _():
        o_ref[0] = x_ref[0] + gate_ref[0] * yt_s[...].T


def _peer_sweep(ht, r2, e2, n1, e1, u, vt, x, gate, tile):
    nb, s, _ = x.shape
    r = gate.shape[1]
    ns = s // tile
    n_items = nb * ns * N_CHUNKS
    last = n_items - 1
    item_a = lambda st: jnp.minimum(st, last)
    item_g = lambda st: jnp.clip(st - 1, 0, last)
    item_y = lambda st: jnp.clip(st - PIPE_DEPTH, 0, last)
    keyed = pl.BlockSpec((PEER_HEADS, PEER_N_KEYS, tile), lambda st: (0, 0, item_g(st) // N_CHUNKS))

    def tok_index(st):
        it = item_y(st) // N_CHUNKS
        return (it // ns, it % ns, 0)

    def mod_index(st):
        it = item_y(st) // N_CHUNKS
        return (it // ns, it % ns if r == s else 0, 0)

    single = pl.Buffered(1)
    return pl.pallas_call(
        _peer_sweep_kernel,
        grid=(n_items + PIPE_DEPTH,),
        in_specs=[pl.BlockSpec((D_MODEL, tile), lambda st: (0, item_a(st) // N_CHUNKS)),
                  keyed, keyed, keyed, keyed,
                  pl.BlockSpec((EXPERT_CHUNK, D_MODEL), lambda st: (item_a(st) % N_CHUNKS, 0)),
                  pl.BlockSpec((D_MODEL, EXPERT_CHUNK), lambda st: (0, item_y(st) % N_CHUNKS)),
                  pl.BlockSpec((1, tile, D_MODEL), tok_index, pipeline_mode=single),
                  pl.BlockSpec((1, tile if r == s else 1, D_MODEL), mod_index, pipeline_mode=single)],
        out_specs=pl.BlockSpec((1, tile, D_MODEL), tok_index),
        out_shape=jax.ShapeDtypeStruct(x.shape, F32),
        scratch_shapes=[pltpu.VMEM((EXPERT_CHUNK, tile), F32), pltpu.VMEM((EXPERT_CHUNK, tile), F32),
                        pltpu.VMEM((EXPERT_CHUNK, tile), BF16), pltpu.VMEM((EXPERT_CHUNK, tile), BF16),
                        pltpu.VMEM((D_MODEL, tile), F32)],
        compiler_params=_cparams(1),
        name="peer_sweep",
    )(ht, r2, e2, n1, e1, u, vt, x, gate)


def _peer(x, shift, scale, gate, gffn, pw, route_tile, sweep_tile):
    routed = _peer_route(x, shift, scale, gffn, pw["wqt"], pw["subk"], route_tile)
    return _peer_sweep(*routed, pw["u"], pw["vt"], x, gate, sweep_tile)


def _rope_tables(pos):
    inv = ROPE_THETA ** (-jnp.arange(0, QK_ROPE_DIM, 2, dtype=F32) / QK_ROPE_DIM)
    ang = pos.astype(F32)[:, None] * inv[None, :]
    c, s, z = jnp.cos(ang), jnp.sin(ang), jnp.zeros_like(ang)
    return jnp.concatenate([c, z, c, z], -1), jnp.concatenate([-s, z, s, z], -1)


def _pad_rope_cols(w):
    z = jnp.zeros(w.shape[:-1] + (ROPE_HALF,), w.dtype)
    return jnp.concatenate([w[..., :ROPE_HALF], z, w[..., ROPE_HALF:], z], -1)


def _mla_weights(w_in, g_q_lat, g_kv_lat, w_uq, w_uk, w_uv, g_qn_nope, g_qn_rope, g_kn_nope,
                 g_kn_rope, w_out):
    lat_end = Q_LORA_RANK + KV_LORA_RANK
    w_in_p = jnp.concatenate([w_in[:, :lat_end], _pad_rope_cols(w_in[:, lat_end:])], -1)
    wq = w_uq.reshape(Q_LORA_RANK, N_HEADS, QK_NOPE_DIM + QK_ROPE_DIM)
    wq = jnp.concatenate([wq[..., :QK_NOPE_DIM], _pad_rope_cols(wq[..., QK_NOPE_DIM:])], -1)
    row = lambda g: g.reshape(1, -1)
    return {
        "w_in": w_in_p.astype(BF16),
        "g_q_lat": row(g_q_lat), "g_kv_lat": row(g_kv_lat), "g_kr": row(_pad_rope_cols(g_kn_rope)),
        "w_uq": wq.reshape(Q_LORA_RANK, N_HEADS * HEAD_PAD).astype(BF16),
        "g_qn": row(g_qn_nope), "g_qr": row(_pad_rope_cols(g_qn_rope)),
        "w_uk": w_uk.reshape(KV_LORA_RANK, N_HEADS * QK_NOPE_DIM).astype(BF16),
        "g_kn": row(g_kn_nope),
        "w_ukt": jnp.transpose(w_uk, (1, 2, 0)).astype(BF16),
        "w_uv": jnp.transpose(w_uv, (1, 0, 2)).astype(BF16),
        "w_out": w_out.astype(BF16),
    }


def _peer_weights(w_q, subkeys, u, v):
    return {
        "wqt": w_q.T.astype(BF16),
        "subk": subkeys.reshape(2 * PEER_HEADS, PEER_N_KEYS, PEER_HALF).astype(BF16),
        "u": u.astype(BF16),
        "vt": v.T.astype(BF16),
    }


PROMPT_TILE = 256
SAMPLE_TILE = 128
PEER_SWEEP_TILE = 512


def kernel(x_prompt, x_sample, c_prompt, c_sample, cache_mla, state_pool, page_table, norm_mix, norm_ffn, w_ada, b_ada, w_mla_in, g_q_lat, g_kv_lat, w_uq, w_uk, w_uv, g_qn_nope, g_qn_rope, g_kn_nope, g_kn_rope, w_mla_out, w_pool, pool_scale, w_peer_q, peer_subkeys, peer_u, peer_v):
    nbp, seq, _ = x_prompt.shape
    nbs = x_sample.shape[0]
    depth = w_ada.shape[0]
    xp = x_prompt
    xs = x_sample.reshape(1, nbs, D_MODEL)

    mods = _ada_params(jnp.concatenate([c_prompt, c_sample], 0), w_ada, b_ada)
    mods = mods.reshape(depth, nbp + nbs, 6, D_MODEL)
    cs_p, sn_p = _rope_tables(jnp.arange(seq))
    cs_s, sn_s = _rope_tables(jnp.full((nbs,), PAST_LEN))
    hsel = (jnp.arange(N_HEADS)[:, None] == jnp.arange(N_HEADS * QK_NOPE_DIM)[None, :] // QK_NOPE_DIM
            ).astype(BF16)

    rows_p, rows_s, pst_p, pst_s = [], [], [], []
    for l in range(depth):
        mp = [mods[l, :nbp, k].reshape(nbp, 1, D_MODEL) for k in range(6)]
        ms = [mods[l, nbp:, k].reshape(1, nbs, D_MODEL) for k in range(6)]
        gmix = norm_mix[l].reshape(1, D_MODEL)
        gffn = norm_ffn[l].reshape(1, D_MODEL)
        i = l // 2
        if l % 2 == 0:
            mw = _mla_weights(w_mla_in[i], g_q_lat[i], g_kv_lat[i], w_uq[i], w_uk[i], w_uv[i],
                              g_qn_nope[i], g_qn_rope[i], g_kn_nope[i], g_kn_rope[i], w_mla_out[i])
            r_p, q, k, v = _mla_project(xp, mp[0], mp[1], gmix, cs_p, sn_p, mw, PROMPT_TILE, False)
            xp = _prompt_attention(q, k, v, xp, mp[2], mw["w_uv"], mw["w_out"], PROMPT_TILE)
            r_s, q, k, v, qa, sself = _mla_project(xs, ms[0], ms[1], gmix, cs_s, sn_s, mw,
                                                   SAMPLE_TILE, True)
            q = q.reshape(nbs, N_HEADS, HEAD_PAD)[:, :, QK_NOPE_DIM:]
            qr = jnp.concatenate([q[..., :ROPE_HALF], q[..., 2 * ROPE_HALF:3 * ROPE_HALF]], -1)
            ctx = _sample_attention(
                cache_mla, i, page_table, qa.reshape(nbs, N_HEADS, KV_LORA_RANK), qr,
                sself.reshape(nbs, 128)[:, :N_HEADS, None], r_s.reshape(nbs, 1, CACHE_ROW)[..., :KV_LORA_RANK],
                hsel, mw["w_uk"])
            xs = _sample_out(ctx.reshape(nbs, N_HEADS * KV_LORA_RANK), xs[0], ms[2][0],
                             mw["w_uv"], mw["w_out"]).reshape(1, nbs, D_MODEL)
            rows_p.append(r_p)
            rows_s.append(r_s.reshape(nbs, 1, CACHE_ROW))
        else:
            wp = w_pool[i].astype(BF16)
            psc = pool_scale[i].reshape(1, D_MODEL)
            xp, st_p = _pool_prompt(xp, mp[0], mp[1], mp[2], gmix, wp, psc, PROMPT_TILE)
            x2, st_s = _pool_sample(xs[0], ms[0][0], ms[1][0], ms[2][0], gmix, wp, psc,
                                    jnp.transpose(state_pool[i], (1, 0, 2)))
            xs = x2.reshape(1, nbs, D_MODEL)
            pst_p.append(st_p)
            pst_s.append(jnp.transpose(st_s, (1, 0, 2)))
        pw = _peer_weights(w_peer_q[l], peer_subkeys[l], peer_u[l], peer_v[l])
        xp = _peer(xp, mp[3], mp[4], mp[5], gffn, pw, PROMPT_TILE, PEER_SWEEP_TILE)
        xs = _peer(xs, ms[3], ms[4], ms[5], gffn, pw, SAMPLE_TILE, SAMPLE_TILE)
    return (xp, xs.reshape(nbs, 1, D_MODEL), jnp.stack(rows_p), jnp.stack(rows_s),
            jnp.stack(pst_p), jnp.stack(pst_s))
```

```python
import functools

import jax
import jax.numpy as jnp
from jax import lax
from jax.experimental import pallas as pl
from jax.experimental.pallas import tpu as pltpu

F32 = jnp.float32
BF16 = jnp.bfloat16

D_MODEL = 1024
EPS = 1e-6
PAST_LEN = 8192
PAGE_SIZE = 128
N_HEADS = 8
QK_NOPE_DIM = 128
QK_ROPE_DIM = 64
ROPE_HALF = QK_ROPE_DIM // 2
ROPE_PAD = 128
HEAD_PAD = QK_NOPE_DIM + ROPE_PAD
V_HEAD_DIM = 128
Q_LORA_RANK = 384
KV_LORA_RANK = 256
CACHE_ROW = KV_LORA_RANK + QK_ROPE_DIM
ROPE_THETA = 10000.0
ATTN_SCALE = (QK_NOPE_DIM + QK_ROPE_DIM) ** -0.5
POOL_WINDOWS = (2, 4, 8, 16)
POOL_GROUP_DIM = D_MODEL // len(POOL_WINDOWS)
POOL_STATE_LEN = max(POOL_WINDOWS) - 1
POOL_HALO = 16
PEER_HEADS = 8
PEER_N_KEYS = 128
PEER_N_EXPERTS = PEER_N_KEYS * PEER_N_KEYS
PEER_TOPK = 16
PEER_HALF = 128
EXPERT_CHUNK = 2048
KEY_ROWS_PER_CHUNK = EXPERT_CHUNK // PEER_N_KEYS

VMEM_LIMIT = 56 * 1024 * 1024

NEG_INF = float("-inf")


def _cparams(n_grid):
    return pltpu.CompilerParams(
        dimension_semantics=("arbitrary",) * n_grid, vmem_limit_bytes=VMEM_LIMIT)


def _rms(x, g, n=None):
    n = x.shape[-1] if n is None else n
    ms = jnp.sum(x * x, axis=-1, keepdims=True) * (1.0 / n)
    return x * lax.rsqrt(ms + EPS) * g


def _bdot(a, b):
    return jnp.dot(a.astype(BF16), b.astype(BF16), preferred_element_type=F32)


def _bdot_nt(a, b):
    return lax.dot_general(a.astype(BF16), b.astype(BF16), (((1,), (1,)), ((), ())),
                           preferred_element_type=F32)


def _mods_kernel(c_ref, w_ref, b_ref, o_ref):
    o_ref[0] = _bdot(jax.nn.silu(c_ref[...]), w_ref[0]) + b_ref[0]


def _ada_params(c_all, w_ada, b_ada):
    n_layers = w_ada.shape[0]
    n_rows = c_all.shape[0]
    n_col = w_ada.shape[2] // D_MODEL
    return pl.pallas_call(
        _mods_kernel,
        grid=(n_layers, n_col),
        in_specs=[
            pl.BlockSpec((n_rows, D_MODEL), lambda l, k: (0, 0)),
            pl.BlockSpec((1, D_MODEL, D_MODEL), lambda l, k: (l, 0, k)),
            pl.BlockSpec((1, 1, D_MODEL), lambda l, k: (l, 0, k)),
        ],
        out_specs=pl.BlockSpec((1, n_rows, D_MODEL), lambda l, k: (l, 0, k)),
        out_shape=jax.ShapeDtypeStruct((n_layers, n_rows, n_col * D_MODEL), F32),
        compiler_params=_cparams(2),
        name="ada_params",
    )(c_all, w_ada, b_ada.reshape(n_layers, 1, -1))


def _rope_pad(x, g, cs, sn):
    y = _rms(x, g, QK_ROPE_DIM)
    return y * cs + pltpu.roll(y, ROPE_PAD // 2, axis=1) * sn


def _mla_proj_kernel(absorb, x_ref, shift_ref, scale_ref, gmix_ref, cs_ref, sn_ref, w_in_ref,
                     g_q_lat_ref, g_kv_lat_ref, g_kr_ref, w_uq_ref, g_qn_ref, g_qr_ref,
                     w_uk_ref, g_kn_ref, *rest):
    if absorb:
        w_ukt_ref, rows_ref, q_ref, k_ref, v_ref, qa_ref, sself_ref = rest
    else:
        rows_ref, q_ref, k_ref, v_ref = rest
    cs = cs_ref[...]
    sn = sn_ref[...]
    h = _rms(x_ref[0], gmix_ref[...]) * (1.0 + scale_ref[0]) + shift_ref[0]
    z = _bdot(h, w_in_ref[...])
    cq = _rms(z[:, :Q_LORA_RANK], g_q_lat_ref[...])
    lat = _rms(z[:, Q_LORA_RANK:Q_LORA_RANK + KV_LORA_RANK], g_kv_lat_ref[...])
    kr = _rope_pad(z[:, Q_LORA_RANK + KV_LORA_RANK:], g_kr_ref[...], cs, sn)
    rows_ref[0, :, :KV_LORA_RANK] = lat
    rows_ref[0, :, KV_LORA_RANK:] = (kr + pltpu.roll(kr, ROPE_PAD - ROPE_HALF, axis=1))[:, :QK_ROPE_DIM]
    v_ref[0] = lat.astype(BF16)
    q = _bdot(cq, w_uq_ref[...])
    kn_all = _bdot(lat, w_uk_ref[...])
    kr_b = kr.astype(BF16)
    if absorb:
        lane = lax.broadcasted_iota(jnp.int32, (x_ref.shape[1], 128), 1)
        sself = jnp.zeros((x_ref.shape[1], 128), F32)
    for hd in range(N_HEADS):
        o = hd * HEAD_PAD
        qn = _rms(q[:, o:o + QK_NOPE_DIM], g_qn_ref[...]) * ATTN_SCALE
        qr = _rope_pad(q[:, o + QK_NOPE_DIM:o + HEAD_PAD], g_qr_ref[...], cs, sn) * ATTN_SCALE
        kn = _rms(kn_all[:, hd * QK_NOPE_DIM:(hd + 1) * QK_NOPE_DIM], g_kn_ref[...])
        qn_b, qr_b, kn_b = qn.astype(BF16), qr.astype(BF16), kn.astype(BF16)
        q_ref[0, :, o:o + QK_NOPE_DIM] = qn_b
        q_ref[0, :, o + QK_NOPE_DIM:o + HEAD_PAD] = qr_b
        k_ref[0, :, o:o + QK_NOPE_DIM] = kn_b
        k_ref[0, :, o + QK_NOPE_DIM:o + HEAD_PAD] = kr_b
        if absorb:
            s = (jnp.sum(qn_b.astype(F32) * kn_b.astype(F32), axis=-1, keepdims=True)
                 + jnp.sum(qr_b.astype(F32) * kr_b.astype(F32), axis=-1, keepdims=True))
            sself = jnp.where(lane == hd, s, sself)
            qa = _bdot(qn * g_kn_ref[...], w_ukt_ref[hd])
            qa_ref[0, :, hd * KV_LORA_RANK:(hd + 1) * KV_LORA_RANK] = qa.astype(BF16)
    if absorb:
        sself_ref[0] = sself


def _mla_project(x, shift, scale, gmix, cs, sn, mw, tile, absorb):
    nb, s, _ = x.shape
    r = shift.shape[1]
    ns = s // tile
    rt = tile if r == s else 1
    tok = lambda bs: pl.BlockSpec(bs, lambda b, i: (b, i, 0))
    mod = pl.BlockSpec((1, rt, D_MODEL), (lambda b, i: (b, i, 0)) if r == s else (lambda b, i: (b, 0, 0)))
    full = lambda a: pl.BlockSpec(a.shape, lambda b, i: (0,) * a.ndim)
    weights = [mw["w_in"], mw["g_q_lat"], mw["g_kv_lat"], mw["g_kr"], mw["w_uq"], mw["g_qn"],
               mw["g_qr"], mw["w_uk"], mw["g_kn"]]
    if absorb:
        weights.append(mw["w_ukt"])
    out_shape = [jax.ShapeDtypeStruct((nb, s, CACHE_ROW), F32),
                 jax.ShapeDtypeStruct((nb, s, N_HEADS * HEAD_PAD), BF16),
                 jax.ShapeDtypeStruct((nb, s, N_HEADS * HEAD_PAD), BF16),
                 jax.ShapeDtypeStruct((nb, s, KV_LORA_RANK), BF16)]
    out_specs = [tok((1, tile, CACHE_ROW)), tok((1, tile, N_HEADS * HEAD_PAD)),
                 tok((1, tile, N_HEADS * HEAD_PAD)), tok((1, tile, KV_LORA_RANK))]
    if absorb:
        out_shape += [jax.ShapeDtypeStruct((nb, s, N_HEADS * KV_LORA_RANK), BF16),
                      jax.ShapeDtypeStruct((nb, s, 128), F32)]
        out_specs += [tok((1, tile, N_HEADS * KV_LORA_RANK)), tok((1, tile, 128))]
    return pl.pallas_call(
        functools.partial(_mla_proj_kernel, absorb),
        grid=(nb, ns),
        in_specs=[tok((1, tile, D_MODEL)), mod, mod, full(gmix),
                  pl.BlockSpec((tile, ROPE_PAD), lambda b, i: (i, 0)),
                  pl.BlockSpec((tile, ROPE_PAD), lambda b, i: (i, 0))] + [full(w) for w in weights],
        out_specs=out_specs,
        out_shape=out_shape,
        compiler_params=_cparams(2),
        name="mla_project_sample" if absorb else "mla_project",
    )(x, shift, scale, gmix, cs, sn, *weights)


def _attn_out(ctx_of_head, x, gate, w_uv_ref, w_out_ref):
    o = jnp.concatenate([_bdot(ctx_of_head(hd), w_uv_ref[hd]) for hd in range(N_HEADS)], axis=-1)
    return x + gate * _bdot(o, w_out_ref[...])


def _flash_kernel(q_ref, k_ref, v_ref, x_ref, gate_ref, w_uv_ref, w_out_ref, o_ref,
                  m_s, l_s, acc_s):
    qi = pl.program_id(1)
    ki = pl.program_id(2)
    tq = q_ref.shape[1]

    @pl.when(ki == 0)
    def _():
        m_s[...] = jnp.full(m_s.shape, NEG_INF, F32)
        l_s[...] = jnp.zeros(l_s.shape, F32)
        acc_s[...] = jnp.zeros(acc_s.shape, F32)

    def step(masked):
        v = v_ref[0]
        if masked:
            row = lax.broadcasted_iota(jnp.int32, (tq, tq), 0)
            col = lax.broadcasted_iota(jnp.int32, (tq, tq), 1)
            keep = col <= row
        for hd in range(N_HEADS):
            o = hd * HEAD_PAD
            s = _bdot_nt(q_ref[0, :, o:o + HEAD_PAD], k_ref[0, :, o:o + HEAD_PAD])
            if masked:
                s = jnp.where(keep, s, NEG_INF)
            m_prev = m_s[hd]
            m_new = jnp.maximum(m_prev, jnp.max(s, axis=-1, keepdims=True))
            alpha = jnp.exp(m_prev - m_new)
            p = jnp.exp(s - m_new)
            l_s[hd] = alpha * l_s[hd] + jnp.sum(p, axis=-1, keepdims=True)
            acc_s[hd] = alpha * acc_s[hd] + _bdot(p, v)
            m_s[hd] = m_new

    @pl.when(ki < qi)
    def _():
        step(False)

    @pl.when(ki == qi)
    def _():
        step(True)
        o_ref[0] = _attn_out(lambda hd: acc_s[hd] / l_s[hd], x_ref[0], gate_ref[0],
                             w_uv_ref, w_out_ref)


def _prompt_attention(q, k, v, x, gate, w_uv, w_out, tile):
    nb, s, _ = x.shape
    nt = s // tile
    full = lambda a: pl.BlockSpec(a.shape, lambda b, i, j: (0,) * a.ndim)
    return pl.pallas_call(
        _flash_kernel,
        grid=(nb, nt, nt),
        in_specs=[
            pl.BlockSpec((1, tile, N_HEADS * HEAD_PAD), lambda b, i, j: (b, i, 0)),
            pl.BlockSpec((1, tile, N_HEADS * HEAD_PAD), lambda b, i, j: (b, jnp.minimum(i, j), 0)),
            pl.BlockSpec((1, tile, KV_LORA_RANK), lambda b, i, j: (b, jnp.minimum(i, j), 0)),
            pl.BlockSpec((1, tile, D_MODEL), lambda b, i, j: (b, i, 0)),
            pl.BlockSpec((1, 1, D_MODEL), lambda b, i, j: (b, 0, 0)),
            full(w_uv), full(w_out),
        ],
        out_specs=pl.BlockSpec((1, tile, D_MODEL), lambda b, i, j: (b, i, 0)),
        out_shape=jax.ShapeDtypeStruct(x.shape, F32),
        scratch_shapes=[pltpu.VMEM((N_HEADS, tile, 1), F32), pltpu.VMEM((N_HEADS, tile, 1), F32),
                        pltpu.VMEM((N_HEADS, tile, KV_LORA_RANK), F32)],
        compiler_params=_cparams(3),
        name="prompt_attention",
    )(q, k, v, x, gate, w_uv, w_out)


PAGES_PER_STEP = 32
PAGES_PER_DOT = 4
QA_ROWS = 16


def _sample_attn_kernel(pt_ref, qa_ref, qr_ref, sself_ref, lat_new_ref, w_ukt_ref, *rest):
    del pt_ref
    pages = rest[:PAGES_PER_STEP]
    ctx_ref, m_s, l_s, acc_s = rest[PAGES_PER_STEP:]
    step = pl.program_id(1)

    @pl.when(step == 0)
    def _():
        m_s[...] = sself_ref[0]
        l_s[...] = jnp.ones(l_s.shape, F32)
        acc_s[...] = jnp.broadcast_to(lat_new_ref[0], acc_s.shape)

    n_key_rows = N_HEADS * QK_NOPE_DIM
    lhs = jnp.concatenate([w_ukt_ref[...], qa_ref[0]], axis=0)
    qr = qr_ref[0]
    head_row = lax.broadcasted_iota(jnp.int32, (N_HEADS, PAGES_PER_DOT * PAGE_SIZE), 0)
    for c in range(PAGES_PER_STEP // PAGES_PER_DOT):
        rows_t = jnp.concatenate([pages[c * PAGES_PER_DOT + p][...] for p in range(PAGES_PER_DOT)], axis=1)
        lat_t = rows_t[:KV_LORA_RANK].astype(BF16)
        kr_t = rows_t[KV_LORA_RANK:].astype(BF16)
        yq = jnp.dot(lhs, lat_t, preferred_element_type=F32)
        ssq = jnp.zeros(head_row.shape, F32)
        for hd in range(N_HEADS):
            y = yq[hd * QK_NOPE_DIM:(hd + 1) * QK_NOPE_DIM]
            ssq = jnp.where(head_row == hd, jnp.sum(y * y, axis=0, keepdims=True), ssq)
        rinv = lax.rsqrt(ssq * (1.0 / QK_NOPE_DIM) + EPS)
        s = yq[n_key_rows:n_key_rows + N_HEADS] * rinv + jnp.dot(qr, kr_t, preferred_element_type=F32)
        m_prev = m_s[...]
        m_new = jnp.maximum(m_prev, jnp.max(s, axis=-1, keepdims=True))
        alpha = jnp.exp(m_prev - m_new)
        p = jnp.exp(s - m_new)
        l_s[...] = alpha * l_s[...] + jnp.sum(p, axis=-1, keepdims=True)
        acc_s[...] = alpha * acc_s[...] + _bdot_nt(p, lat_t)
        m_s[...] = m_new

    @pl.when(step == pl.num_programs(1) - 1)
    def _():
        ctx_ref[0] = acc_s[...] / l_s[...]


def _sample_attention(cache_t, layer, page_table, qa, qr, sself, lat_new, w_ukt):
    nb, n_pages = page_table.shape
    n_steps = n_pages // PAGES_PER_STEP
    per_b = lambda a: pl.BlockSpec((1,) + a.shape[1:], lambda b, s, pt: (b,) + (0,) * (a.ndim - 1))
    full = lambda a: pl.BlockSpec(a.shape, lambda b, s, pt: (0,) * a.ndim)

    def page_spec(j):
        return pl.BlockSpec((None, None, CACHE_ROW, PAGE_SIZE),
                            lambda b, s, pt: (layer, pt[b * n_pages + s * PAGES_PER_STEP + j], 0, 0))

    grid_spec = pltpu.PrefetchScalarGridSpec(
        num_scalar_prefetch=1,
        grid=(nb, n_steps),
        in_specs=[per_b(qa), per_b(qr), per_b(sself), per_b(lat_new), full(w_ukt)]
        + [page_spec(j) for j in range(PAGES_PER_STEP)],
        out_specs=pl.BlockSpec((1, N_HEADS, KV_LORA_RANK), lambda b, s, pt: (b, 0, 0)),
        scratch_shapes=[pltpu.VMEM((N_HEADS, 1), F32), pltpu.VMEM((N_HEADS, 1), F32),
                        pltpu.VMEM((N_HEADS, KV_LORA_RANK), F32)],
    )
    return pl.pallas_call(
        _sample_attn_kernel,
        grid_spec=grid_spec,
        out_shape=jax.ShapeDtypeStruct((nb, N_HEADS, KV_LORA_RANK), F32),
        compiler_params=_cparams(2),
        name="sample_attention",
    )(page_table.reshape(-1), qa, qr, sself, lat_new, w_ukt, *([cache_t] * PAGES_PER_STEP))


def _sample_out_kernel(ctx_ref, x_ref, gate_ref, w_uv_ref, w_out_ref, o_ref):
    o_ref[...] = _attn_out(lambda hd: ctx_ref[:, hd * KV_LORA_RANK:(hd + 1) * KV_LORA_RANK],
                           x_ref[...], gate_ref[...], w_uv_ref, w_out_ref)


def _sample_out(ctx, x, gate, w_uv, w_out):
    return pl.pallas_call(
        _sample_out_kernel,
        out_shape=jax.ShapeDtypeStruct(x.shape, F32),
        compiler_params=pltpu.CompilerParams(vmem_limit_bytes=VMEM_LIMIT),
        name="sample_attention_out",
    )(ctx, x, gate, w_uv, w_out)


def _pool_mix(window_sum, h, cnt, w_pool_ref, pscale):
    parts = []
    for gi, w in enumerate(POOL_WINDOWS):
        ch = slice(gi * POOL_GROUP_DIM, (gi + 1) * POOL_GROUP_DIM)
        acc = h[:, ch]
        for kk in range(1, w):
            acc = acc + window_sum(kk, ch)
        d = acc / cnt[gi] - h[:, ch]
        parts.append(_bdot(d, w_pool_ref[gi]))
    return jnp.concatenate(parts, axis=-1) * pscale


def _pool_prompt_kernel(x_ref, shift_ref, scale_ref, gate_ref, gmix_ref, w_pool_ref, pscale_ref,
                        o_ref, st_ref, ext_s):
    si = pl.program_id(1)
    t = x_ref.shape[1]

    @pl.when(si == 0)
    def _():
        ext_s[0:POOL_HALO] = jnp.zeros((POOL_HALO, D_MODEL), F32)

    @pl.when(si > 0)
    def _():
        ext_s[0:POOL_HALO] = ext_s[t:t + POOL_HALO]

    x = x_ref[0]
    h = _rms(x, gmix_ref[...]) * (1.0 + scale_ref[0]) + shift_ref[0]
    ext_s[POOL_HALO:POOL_HALO + t] = h
    pos = si * t + lax.broadcasted_iota(jnp.int32, (t, 1), 0)
    cnt = [jnp.minimum(pos + 1, w).astype(F32) for w in POOL_WINDOWS]
    y = _pool_mix(lambda kk, ch: ext_s[POOL_HALO - kk:POOL_HALO - kk + t, ch], h, cnt,
                  w_pool_ref, pscale_ref[...])
    o_ref[0] = x + gate_ref[0] * y

    @pl.when(si == pl.num_programs(1) - 1)
    def _():
        st_ref[0] = ext_s[POOL_HALO + t - POOL_STATE_LEN:POOL_HALO + t]


def _pool_prompt(x, shift, scale, gate, gmix, w_pool, pscale, tile):
    nb, s, _ = x.shape
    tok = pl.BlockSpec((1, tile, D_MODEL), lambda b, i: (b, i, 0))
    mod = pl.BlockSpec((1, 1, D_MODEL), lambda b, i: (b, 0, 0))
    full = lambda a: pl.BlockSpec(a.shape, lambda b, i: (0,) * a.ndim)
    return pl.pallas_call(
        _pool_prompt_kernel,
        grid=(nb, s // tile),
        in_specs=[tok, mod, mod, mod, full(gmix), full(w_pool), full(pscale)],
        out_specs=[tok, pl.BlockSpec((1, POOL_STATE_LEN, D_MODEL), lambda b, i: (b, 0, 0))],
        out_shape=[jax.ShapeDtypeStruct(x.shape, F32),
                   jax.ShapeDtypeStruct((nb, POOL_STATE_LEN, D_MODEL), F32)],
        scratch_shapes=[pltpu.VMEM((POOL_HALO + tile, D_MODEL), F32)],
        compiler_params=_cparams(2),
        name="pool_prompt",
    )(x, shift, scale, gate, gmix, w_pool, pscale)


def _pool_sample_kernel(x_ref, shift_ref, scale_ref, gate_ref, gmix_ref, w_pool_ref, pscale_ref,
                        state_ref, o_ref, st_ref):
    x = x_ref[...]
    h = _rms(x, gmix_ref[...]) * (1.0 + scale_ref[...]) + shift_ref[...]
    cnt = [float(w) for w in POOL_WINDOWS]
    y = _pool_mix(lambda kk, ch: state_ref[POOL_STATE_LEN - kk, :, ch], h, cnt,
                  w_pool_ref, pscale_ref[...])
    o_ref[...] = x + gate_ref[...] * y
    for r in range(POOL_STATE_LEN - 1):
        st_ref[r] = state_ref[r + 1]
    st_ref[POOL_STATE_LEN - 1] = h


def _pool_sample(x, shift, scale, gate, gmix, w_pool, pscale, state_t):
    return pl.pallas_call(
        _pool_sample_kernel,
        out_shape=[jax.ShapeDtypeStruct(x.shape, F32), jax.ShapeDtypeStruct(state_t.shape, F32)],
        compiler_params=pltpu.CompilerParams(vmem_limit_bytes=VMEM_LIMIT),
        name="pool_sample",
    )(x, shift, scale, gate, gmix, w_pool, pscale, state_t)


def _cand_rows(b):
    return PEER_TOPK // (b + 1)


NOT_TOP = 64.0
LANES = 128
SUBLANES = 8


def _sort_pairs(n):
    pairs, p = [], 1
    while p < n:
        k = p
        while k >= 1:
            for j in range(k % p, n - k, 2 * k):
                for i in range(min(k, n - j - k)):
                    if (i + j) // (2 * p) == (i + j + k) // (2 * p):
                        pairs.append((i + j, i + j + k))
            k //= 2
        p *= 2
    return pairs


def _bitonic_merge_pairs(n):
    pairs, k = [], n // 2
    while k >= 1:
        pairs += [(i, i + k) for i in range(n) if not i & k]
        k //= 2
    return pairs


def _compare_exchange(v, pairs):
    for i, j in pairs:
        v[i], v[j] = jnp.maximum(v[i], v[j]), jnp.minimum(v[i], v[j])
    return v


def _top16_sorted(s):
    v = [s[r * SUBLANES:(r + 1) * SUBLANES] for r in range(PEER_TOPK)]
    v = _compare_exchange(v, _sort_pairs(PEER_TOPK))
    for shift in (4, 2, 1):
        v = [jnp.maximum(v[i], pltpu.roll(v[PEER_TOPK - 1 - i], shift, axis=0)) for i in range(PEER_TOPK)]
        v = _compare_exchange(v, _bitonic_merge_pairs(PEER_TOPK))
    return v


def _kth_largest(work, k):
    for _ in range(k):
        m = jnp.max(work, axis=0, keepdims=True)
        work = jnp.where(work == m, NEG_INF, work)
    return m


def _prefix_count(x, thresholds, above):
    parts = []
    for r in range(x.shape[0] // SUBLANES):
        xr = x[r * SUBLANES:(r + 1) * SUBLANES]
        cnt = jnp.zeros(xr.shape, F32)
        for b, th in enumerate(thresholds):
            cnt = jnp.where(th > xr if above else xr >= th, float(b + 1), cnt)
        parts.append(cnt)
    return jnp.concatenate(parts, axis=0)


def _dup_bf16(x):
    hi = pltpu.bitcast(x.astype(BF16).astype(F32), jnp.uint32)
    return hi | (hi >> 16)


def _pack_pairs(x):
    return pltpu.bitcast(x.astype(BF16), jnp.uint32)


def _peer_route_kernel(x_ref, shift_ref, scale_ref, gffn_ref, wqt_ref, subk_ref,
                       ht_ref, r2_ref, e2_ref, n1_ref, e1_ref):
    t = x_ref.shape[1]
    h = _rms(x_ref[0], gffn_ref[...]) * (1.0 + scale_ref[0]) + shift_ref[0]
    ht = h.T.astype(BF16)
    ht_ref[...] = pltpu.bitcast(ht, jnp.uint32)
    qt = jnp.dot(wqt_ref[...], ht, preferred_element_type=F32).astype(BF16)
    row8 = lax.broadcasted_iota(jnp.int32, (SUBLANES, LANES), 0)
    for hd in range(PEER_HEADS):
        sc = [jnp.dot(subk_ref[2 * hd + p], qt[(2 * hd + p) * PEER_HALF:(2 * hd + p + 1) * PEER_HALF],
                      preferred_element_type=F32) for p in range(2)]
        for lt in range(t // LANES):
            ln = slice(lt * LANES, (lt + 1) * LANES)
            s1, s2 = sc[0][:, ln], sc[1][:, ln]
            top1, top2 = _top16_sorted(s1), _top16_sorted(s2)
            v1_lo, v1_hi = top1[0], top1[SUBLANES]
            for a in range(1, SUBLANES):
                v1_lo = jnp.where(row8 == a, top1[a], v1_lo)
                v1_hi = jnp.where(row8 == a, top1[SUBLANES + a], v1_hi)
            cand = [v1_lo + top2[0], v1_hi + top2[0]]
            for b in range(1, PEER_TOPK):
                cand.append(jnp.where(row8 < _cand_rows(b), v1_lo + top2[b], NEG_INF))
            cand = jnp.concatenate(cand, axis=0)
            tau = _kth_largest(cand, PEER_TOPK)
            cmax = top1[0] + top2[0]
            z = jnp.sum(jnp.where(cand >= tau, jnp.exp(cand - cmax[:1]), 0.0), axis=0, keepdims=True)
            theta = []
            for b in range(PEER_TOPK):
                lo = jnp.where(v1_lo + top2[b] >= tau, v1_lo, -NEG_INF)
                hi = jnp.where(v1_hi + top2[b] >= tau, v1_hi, -NEG_INF)
                theta.append(jnp.min(jnp.minimum(lo, hi), axis=0, keepdims=True))
            n1 = _prefix_count(s1, theta, above=False)
            rank2 = _prefix_count(s2, top2, above=True)
            rank2 = jnp.where(rank2 >= PEER_TOPK, NOT_TOP, rank2)
            r2_ref[hd, :, ln] = _pack_pairs(rank2)
            e2_ref[hd, :, ln] = _pack_pairs(jnp.exp(s2 - top2[0][:1]))
            n1_ref[hd, lt] = _dup_bf16(n1)
            e1_ref[hd, lt] = _dup_bf16(jnp.exp(s1 - top1[0][:1]) * (0.5 / z))


def _peer_route(x, shift, scale, gffn, wqt, subk, tile):
    nb, s, _ = x.shape
    r = shift.shape[1]
    ns = s // tile
    n = nb * s
    rt = tile if r == s else 1
    tok = pl.BlockSpec((1, tile, D_MODEL), lambda b, i: (b, i, 0))
    mod = pl.BlockSpec((1, rt, D_MODEL), (lambda b, i: (b, i, 0)) if r == s else (lambda b, i: (b, 0, 0)))
    full = lambda a: pl.BlockSpec(a.shape, lambda b, i: (0,) * a.ndim)
    u32 = jnp.uint32
    lane_tiles = tile // LANES
    pairs = pl.BlockSpec((PEER_HEADS, PEER_N_KEYS // 2, tile), lambda b, i: (0, 0, b * ns + i))
    rows = pl.BlockSpec((PEER_HEADS, lane_tiles, PEER_N_KEYS, LANES), lambda b, i: (0, b * ns + i, 0, 0))
    pairs_shape = jax.ShapeDtypeStruct((PEER_HEADS, PEER_N_KEYS // 2, n), u32)
    rows_shape = jax.ShapeDtypeStruct((PEER_HEADS, n // LANES, PEER_N_KEYS, LANES), u32)
    return pl.pallas_call(
        _peer_route_kernel,
        grid=(nb, ns),
        in_specs=[tok, mod, mod, full(gffn), full(wqt), full(subk)],
        out_specs=[pl.BlockSpec((D_MODEL // 2, tile), lambda b, i: (0, b * ns + i)),
                   pairs, pairs, rows, rows],
        out_shape=[jax.ShapeDtypeStruct((D_MODEL // 2, n), u32), pairs_shape, pairs_shape,
                   rows_shape, rows_shape],
        compiler_params=_cparams(2),
        name="peer_route",
    )(x, shift, scale, gffn, wqt, subk)


N_CHUNKS = PEER_N_EXPERTS // EXPERT_CHUNK
GATE_ROWS = 32
OUT_DEPTH = 512
OUT_ROWS = 512
KEYS_PER_PIECE = 2


def _peer_sweep_kernel(ht_ref, r2_ref, e2_ref, n1_ref, e1_ref, u_ref, vt_ref, x_ref, gate_ref, o_ref,
                       ga_s, yt_s):
    j = pl.program_id(2)
    t = ht_ref.shape[1]

    @pl.when(j == 0)
    def _():
        yt_s[...] = jnp.zeros(yt_s.shape, F32)

    ht = pltpu.bitcast(ht_ref[...], BF16)
    zero = jnp.zeros((GATE_ROWS, LANES), BF16)
    n_groups = PEER_N_KEYS // GATE_ROWS

    def gate_piece(k):
        n_rows = KEYS_PER_PIECE * PEER_N_KEYS
        u = pltpu.bitcast(u_ref[k * n_rows // 2:(k + 1) * n_rows // 2, :], BF16)
        act = jnp.dot(u, ht, preferred_element_type=F32)
        ga_s[k * n_rows:(k + 1) * n_rows, :] = (act * (1.0 + lax.erf(act * (2.0 ** -0.5)))).astype(BF16)
        for lt in range(t // LANES):
            ln = slice(lt * LANES, (lt + 1) * LANES)
            g = [[zero] * n_groups for _ in range(KEYS_PER_PIECE)]
            for hd in range(PEER_HEADS):
                n1, e1 = [], []
                for kk in range(KEYS_PER_PIECE):
                    row = pl.ds(j * KEY_ROWS_PER_CHUNK + k * KEYS_PER_PIECE + kk, 1)
                    spread = lambda ref: pltpu.bitcast(
                        jnp.broadcast_to(ref[hd, lt, row, :], (GATE_ROWS // 2, LANES)), BF16)
                    n1.append(spread(n1_ref))
                    e1.append(spread(e1_ref))
                for sg in range(n_groups):
                    pairs = slice(sg * GATE_ROWS // 2, (sg + 1) * GATE_ROWS // 2)
                    r2 = pltpu.bitcast(r2_ref[hd, pairs, ln], BF16)
                    e2 = pltpu.bitcast(e2_ref[hd, pairs, ln], BF16)
                    for kk in range(KEYS_PER_PIECE):
                        g[kk][sg] = g[kk][sg] + jnp.where(r2 < n1[kk], e2 * e1[kk], zero)
            for kk in range(KEYS_PER_PIECE):
                for sg in range(n_groups):
                    r0 = k * n_rows + kk * PEER_N_KEYS + sg * GATE_ROWS
                    ga_s[r0:r0 + GATE_ROWS, ln] = ga_s[r0:r0 + GATE_ROWS, ln] * g[kk][sg]

    def out_piece(group, m):
        cols = slice(group * OUT_DEPTH, (group + 1) * OUT_DEPTH)
        rows = slice(m * OUT_ROWS, (m + 1) * OUT_ROWS)
        v = pltpu.bitcast(vt_ref[m * OUT_ROWS // 2:(m + 1) * OUT_ROWS // 2, cols], BF16)
        yt_s[rows, :] += jnp.dot(v, ga_s[cols, :], preferred_element_type=F32)

    pieces_per_group = OUT_DEPTH // (KEYS_PER_PIECE * PEER_N_KEYS)
    n_pieces = KEY_ROWS_PER_CHUNK // KEYS_PER_PIECE
    pending = []
    for k in range(n_pieces):
        gate_piece(k)
        if pending:
            out_piece(*pending.pop(0))
        if (k + 1) % pieces_per_group == 0:
            pending += [(k // pieces_per_group, m) for m in range(D_MODEL // OUT_ROWS)]
    for piece in pending:
        out_piece(*piece)

    @pl.when(j == pl.num_programs(2) - 1)
    def _():
        o_ref[0] = x_ref[0] + gate_ref[0] * yt_s[...].T


def _peer_sweep(ht, r2, e2, n1, e1, u, vt, x, gate, tile):
    nb, s, _ = x.shape
    r = gate.shape[1]
    ns = s // tile
    pairs = pl.BlockSpec((PEER_HEADS, PEER_N_KEYS // 2, tile), lambda b, i, j: (0, 0, b * ns + i))
    rows = pl.BlockSpec((PEER_HEADS, tile // LANES, PEER_N_KEYS, LANES),
                        lambda b, i, j: (0, b * ns + i, 0, 0))
    tok = pl.BlockSpec((1, tile, D_MODEL), lambda b, i, j: (b, i, 0))
    mod = pl.BlockSpec((1, tile if r == s else 1, D_MODEL),
                       (lambda b, i, j: (b, i, 0)) if r == s else (lambda b, i, j: (b, 0, 0)))
    return pl.pallas_call(
        _peer_sweep_kernel,
        grid=(nb, ns, N_CHUNKS),
        in_specs=[pl.BlockSpec((D_MODEL // 2, tile), lambda b, i, j: (0, b * ns + i)),
                  pairs, pairs, rows, rows,
                  pl.BlockSpec((EXPERT_CHUNK // 2, D_MODEL), lambda b, i, j: (j, 0)),
                  pl.BlockSpec((D_MODEL // 2, EXPERT_CHUNK), lambda b, i, j: (0, j)),
                  tok, mod],
        out_specs=tok,
        out_shape=jax.ShapeDtypeStruct(x.shape, F32),
        scratch_shapes=[pltpu.VMEM((EXPERT_CHUNK, tile), BF16), pltpu.VMEM((D_MODEL, tile), F32)],
        compiler_params=_cparams(3),
        name="peer_sweep",
    )(ht, r2, e2, n1, e1, u, vt, x, gate)


def _peer(x, shift, scale, gate, gffn, pw, route_tile, sweep_tile):
    routed = _peer_route(x, shift, scale, gffn, pw["wqt"], pw["subk"], route_tile)
    return _peer_sweep(*routed, pw["u"], pw["vt"], x, gate, sweep_tile)


def _rope_tables(pos):
    inv = ROPE_THETA ** (-jnp.arange(0, QK_ROPE_DIM, 2, dtype=F32) / QK_ROPE_DIM)
    ang = pos.astype(F32)[:, None] * inv[None, :]
    c, s, z = jnp.cos(ang), jnp.sin(ang), jnp.zeros_like(ang)
    return jnp.concatenate([c, z, c, z], -1), jnp.concatenate([-s, z, s, z], -1)


def _pad_rope_cols(w):
    z = jnp.zeros(w.shape[:-1] + (ROPE_HALF,), w.dtype)
    return jnp.concatenate([w[..., :ROPE_HALF], z, w[..., ROPE_HALF:], z], -1)


def _mla_weights(w_in, g_q_lat, g_kv_lat, w_uq, w_uk, w_uv, g_qn_nope, g_qn_rope, g_kn_nope,
                 g_kn_rope, w_out):
    lat_end = Q_LORA_RANK + KV_LORA_RANK
    w_in_p = jnp.concatenate([w_in[:, :lat_end], _pad_rope_cols(w_in[:, lat_end:])], -1)
    wq = w_uq.reshape(Q_LORA_RANK, N_HEADS, QK_NOPE_DIM + QK_ROPE_DIM)
    wq = jnp.concatenate([wq[..., :QK_NOPE_DIM], _pad_rope_cols(wq[..., QK_NOPE_DIM:])], -1)
    row = lambda g: g.reshape(1, -1)
    return {
        "w_in": w_in_p.astype(BF16),
        "g_q_lat": row(g_q_lat), "g_kv_lat": row(g_kv_lat), "g_kr": row(_pad_rope_cols(g_kn_rope)),
        "w_uq": wq.reshape(Q_LORA_RANK, N_HEADS * HEAD_PAD).astype(BF16),
        "g_qn": row(g_qn_nope), "g_qr": row(_pad_rope_cols(g_qn_rope)),
        "w_uk": w_uk.reshape(KV_LORA_RANK, N_HEADS * QK_NOPE_DIM).astype(BF16),
        "g_kn": row(g_kn_nope),
        "w_ukt": jnp.transpose(w_uk, (1, 2, 0)).astype(BF16),
        "w_uv": jnp.transpose(w_uv, (1, 0, 2)).astype(BF16),
        "w_out": w_out.astype(BF16),
    }


def _pack_row_pairs(w):
    r, c = w.shape
    return lax.bitcast_convert_type(jnp.transpose(w.reshape(r // 2, 2, c), (0, 2, 1)), jnp.uint32)


def _peer_weights(w_q, subkeys, u, v):
    return {
        "wqt": w_q.T.astype(BF16),
        "subk": subkeys.reshape(2 * PEER_HEADS, PEER_N_KEYS, PEER_HALF).astype(BF16),
        "u": _pack_row_pairs(u.astype(BF16)),
        "vt": _pack_row_pairs(v.T.astype(BF16)),
    }


PROMPT_TILE = 256
SAMPLE_TILE = 128
PEER_SWEEP_TILE = 512
ATTN_TILE = 512


def kernel(x_prompt, x_sample, c_prompt, c_sample, cache_mla, state_pool, page_table, norm_mix, norm_ffn, w_ada, b_ada, w_mla_in, g_q_lat, g_kv_lat, w_uq, w_uk, w_uv, g_qn_nope, g_qn_rope, g_kn_nope, g_kn_rope, w_mla_out, w_pool, pool_scale, w_peer_q, peer_subkeys, peer_u, peer_v):
    nbp, seq, _ = x_prompt.shape
    nbs = x_sample.shape[0]
    depth = w_ada.shape[0]
    xp = x_prompt
    xs = x_sample.reshape(1, nbs, D_MODEL)

    mods = _ada_params(jnp.concatenate([c_prompt, c_sample], 0), w_ada, b_ada)
    mods = mods.reshape(depth, nbp + nbs, 6, D_MODEL)
    cs_p, sn_p = _rope_tables(jnp.arange(seq))
    cs_s, sn_s = _rope_tables(jnp.full((nbs,), PAST_LEN))
    cache_t = jnp.swapaxes(cache_mla, 2, 3)

    rows_p, rows_s, pst_p, pst_s = [], [], [], []
    for l in range(depth):
        mp = [mods[l, :nbp, k].reshape(nbp, 1, D_MODEL) for k in range(6)]
        ms = [mods[l, nbp:, k].reshape(1, nbs, D_MODEL) for k in range(6)]
        gmix = norm_mix[l].reshape(1, D_MODEL)
        gffn = norm_ffn[l].reshape(1, D_MODEL)
        i = l // 2
        if l % 2 == 0:
            mw = _mla_weights(w_mla_in[i], g_q_lat[i], g_kv_lat[i], w_uq[i], w_uk[i], w_uv[i],
                              g_qn_nope[i], g_qn_rope[i], g_kn_nope[i], g_kn_rope[i], w_mla_out[i])
            r_p, q, k, v = _mla_project(xp, mp[0], mp[1], gmix, cs_p, sn_p, mw, PROMPT_TILE, False)
            xp = _prompt_attention(q, k, v, xp, mp[2], mw["w_uv"], mw["w_out"], ATTN_TILE)
            r_s, q, k, v, qa, sself = _mla_project(xs, ms[0], ms[1], gmix, cs_s, sn_s, mw,
                                                   SAMPLE_TILE, True)
            q = q.reshape(nbs, N_HEADS, HEAD_PAD)[:, :, QK_NOPE_DIM:]
            qr = jnp.concatenate([q[..., :ROPE_HALF], q[..., 2 * ROPE_HALF:3 * ROPE_HALF]], -1)
            qa = jnp.pad(qa.reshape(nbs, N_HEADS, KV_LORA_RANK), ((0, 0), (0, QA_ROWS - N_HEADS), (0, 0)))
            ctx = _sample_attention(
                cache_t, i, page_table, qa, qr, sself.reshape(nbs, 128)[:, :N_HEADS, None],
                r_s.reshape(nbs, 1, CACHE_ROW)[..., :KV_LORA_RANK],
                mw["w_ukt"].reshape(N_HEADS * QK_NOPE_DIM, KV_LORA_RANK))
            xs = _sample_out(ctx.reshape(nbs, N_HEADS * KV_LORA_RANK), xs[0], ms[2][0],
                             mw["w_uv"], mw["w_out"]).reshape(1, nbs, D_MODEL)
            rows_p.append(r_p)
            rows_s.append(r_s.reshape(nbs, 1, CACHE_ROW))
        else:
            wp = w_pool[i].astype(BF16)
            psc = pool_scale[i].reshape(1, D_MODEL)
            xp, st_p = _pool_prompt(xp, mp[0], mp[1], mp[2], gmix, wp, psc, PROMPT_TILE)
            x2, st_s = _pool_sample(xs[0], ms[0][0], ms[1][0], ms[2][0], gmix, wp, psc,
                                    jnp.transpose(state_pool[i], (1, 0, 2)))
            xs = x2.reshape(1, nbs, D_MODEL)
            pst_p.append(st_p)
            pst_s.append(jnp.transpose(st_s, (1, 0, 2)))
        pw = _peer_weights(w_peer_q[l], peer_subkeys[l], peer_u[l], peer_v[l])
        xp = _peer(xp, mp[3], mp[4], mp[5], gffn, pw, PROMPT_TILE, PEER_SWEEP_TILE)
        xs = _peer(xs, ms[3], ms[4], ms[5], gffn, pw, SAMPLE_TILE, SAMPLE_TILE)
    return (xp, xs.reshape(nbs, 1, D_MODEL), jnp.stack(rows_p), jnp.stack(rows_s),
            jnp.stack(pst_p), jnp.stack(pst_s))
```

```python
import functools

import jax
import jax.numpy as jnp
from jax import lax
from jax.experimental import pallas as pl
from jax.experimental.pallas import tpu as pltpu

F32 = jnp.float32
BF16 = jnp.bfloat16

D_MODEL = 1024
EPS = 1e-6
PAST_LEN = 8192
PAGE_SIZE = 128
N_HEADS = 8
QK_NOPE_DIM = 128
QK_ROPE_DIM = 64
ROPE_HALF = QK_ROPE_DIM // 2
ROPE_PAD = 128
HEAD_PAD = QK_NOPE_DIM + ROPE_PAD
V_HEAD_DIM = 128
Q_LORA_RANK = 384
KV_LORA_RANK = 256
CACHE_ROW = KV_LORA_RANK + QK_ROPE_DIM
ROPE_THETA = 10000.0
ATTN_SCALE = (QK_NOPE_DIM + QK_ROPE_DIM) ** -0.5
POOL_WINDOWS = (2, 4, 8, 16)
POOL_GROUP_DIM = D_MODEL // len(POOL_WINDOWS)
POOL_STATE_LEN = max(POOL_WINDOWS) - 1
POOL_HALO = 16
PEER_HEADS = 8
PEER_N_KEYS = 128
PEER_N_EXPERTS = PEER_N_KEYS * PEER_N_KEYS
PEER_TOPK = 16
PEER_HALF = 128
EXPERT_CHUNK = 2048
KEY_ROWS_PER_CHUNK = EXPERT_CHUNK // PEER_N_KEYS

VMEM_LIMIT = 56 * 1024 * 1024

NEG_INF = float("-inf")


def _cparams(n_grid):
    return pltpu.CompilerParams(
        dimension_semantics=("arbitrary",) * n_grid, vmem_limit_bytes=VMEM_LIMIT)


def _rms(x, g, n=None):
    n = x.shape[-1] if n is None else n
    ms = jnp.sum(x * x, axis=-1, keepdims=True) * (1.0 / n)
    return x * lax.rsqrt(ms + EPS) * g


def _bdot(a, b):
    return jnp.dot(a.astype(BF16), b.astype(BF16), preferred_element_type=F32)


def _bdot_nt(a, b):
    return lax.dot_general(a.astype(BF16), b.astype(BF16), (((1,), (1,)), ((), ())),
                           preferred_element_type=F32)


def _mods_kernel(c_ref, w_ref, b_ref, o_ref):
    o_ref[0] = _bdot(jax.nn.silu(c_ref[...]), w_ref[0]) + b_ref[0]


def _ada_params(c_all, w_ada, b_ada):
    n_layers = w_ada.shape[0]
    n_rows = c_all.shape[0]
    n_col = w_ada.shape[2] // D_MODEL
    return pl.pallas_call(
        _mods_kernel,
        grid=(n_layers, n_col),
        in_specs=[
            pl.BlockSpec((n_rows, D_MODEL), lambda l, k: (0, 0)),
            pl.BlockSpec((1, D_MODEL, D_MODEL), lambda l, k: (l, 0, k)),
            pl.BlockSpec((1, 1, D_MODEL), lambda l, k: (l, 0, k)),
        ],
        out_specs=pl.BlockSpec((1, n_rows, D_MODEL), lambda l, k: (l, 0, k)),
        out_shape=jax.ShapeDtypeStruct((n_layers, n_rows, n_col * D_MODEL), F32),
        compiler_params=_cparams(2),
        name="ada_params",
    )(c_all, w_ada, b_ada.reshape(n_layers, 1, -1))


def _rope_pad(x, g, cs, sn):
    y = _rms(x, g, QK_ROPE_DIM)
    return y * cs + pltpu.roll(y, ROPE_PAD // 2, axis=1) * sn


def _mla_proj_kernel(absorb, x_ref, shift_ref, scale_ref, gmix_ref, cs_ref, sn_ref, w_in_ref,
                     g_q_lat_ref, g_kv_lat_ref, g_kr_ref, w_uq_ref, g_qn_ref, g_qr_ref,
                     w_uk_ref, g_kn_ref, *rest):
    if absorb:
        w_ukt_ref, rows_ref, q_ref, k_ref, v_ref, qa_ref, sself_ref = rest
    else:
        rows_ref, q_ref, k_ref, v_ref = rest
    cs = cs_ref[...]
    sn = sn_ref[...]
    h = _rms(x_ref[0], gmix_ref[...]) * (1.0 + scale_ref[0]) + shift_ref[0]
    z = _bdot(h, w_in_ref[...])
    cq = _rms(z[:, :Q_LORA_RANK], g_q_lat_ref[...])
    lat = _rms(z[:, Q_LORA_RANK:Q_LORA_RANK + KV_LORA_RANK], g_kv_lat_ref[...])
    kr = _rope_pad(z[:, Q_LORA_RANK + KV_LORA_RANK:], g_kr_ref[...], cs, sn)
    rows_ref[0, :, :KV_LORA_RANK] = lat
    rows_ref[0, :, KV_LORA_RANK:] = (kr + pltpu.roll(kr, ROPE_PAD - ROPE_HALF, axis=1))[:, :QK_ROPE_DIM]
    v_ref[0] = lat.astype(BF16)
    q = _bdot(cq, w_uq_ref[...])
    kn_all = _bdot(lat, w_uk_ref[...])
    kr_b = kr.astype(BF16)
    if absorb:
        lane = lax.broadcasted_iota(jnp.int32, (x_ref.shape[1], 128), 1)
        sself = jnp.zeros((x_ref.shape[1], 128), F32)
    for hd in range(N_HEADS):
        o = hd * HEAD_PAD
        qn = _rms(q[:, o:o + QK_NOPE_DIM], g_qn_ref[...]) * ATTN_SCALE
        qr = _rope_pad(q[:, o + QK_NOPE_DIM:o + HEAD_PAD], g_qr_ref[...], cs, sn) * ATTN_SCALE
        kn = _rms(kn_all[:, hd * QK_NOPE_DIM:(hd + 1) * QK_NOPE_DIM], g_kn_ref[...])
        qn_b, qr_b, kn_b = qn.astype(BF16), qr.astype(BF16), kn.astype(BF16)
        q_ref[0, :, o:o + QK_NOPE_DIM] = qn_b
        q_ref[0, :, o + QK_NOPE_DIM:o + HEAD_PAD] = qr_b
        k_ref[0, :, o:o + QK_NOPE_DIM] = kn_b
        k_ref[0, :, o + QK_NOPE_DIM:o + HEAD_PAD] = kr_b
        if absorb:
            s = (jnp.sum(qn_b.astype(F32) * kn_b.astype(F32), axis=-1, keepdims=True)
                 + jnp.sum(qr_b.astype(F32) * kr_b.astype(F32), axis=-1, keepdims=True))
            sself = jnp.where(lane == hd, s, sself)
            qa = _bdot(qn * g_kn_ref[...], w_ukt_ref[hd])
            qa_ref[0, :, hd * KV_LORA_RANK:(hd + 1) * KV_LORA_RANK] = qa.astype(BF16)
    if absorb:
        sself_ref[0] = sself


def _mla_project(x, shift, scale, gmix, cs, sn, mw, tile, absorb):
    nb, s, _ = x.shape
    r = shift.shape[1]
    ns = s // tile
    rt = tile if r == s else 1
    tok = lambda bs: pl.BlockSpec(bs, lambda b, i: (b, i, 0))
    mod = pl.BlockSpec((1, rt, D_MODEL), (lambda b, i: (b, i, 0)) if r == s else (lambda b, i: (b, 0, 0)))
    full = lambda a: pl.BlockSpec(a.shape, lambda b, i: (0,) * a.ndim)
    weights = [mw["w_in"], mw["g_q_lat"], mw["g_kv_lat"], mw["g_kr"], mw["w_uq"], mw["g_qn"],
               mw["g_qr"], mw["w_uk"], mw["g_kn"]]
    if absorb:
        weights.append(mw["w_ukt"])
    out_shape = [jax.ShapeDtypeStruct((nb, s, CACHE_ROW), F32),
                 jax.ShapeDtypeStruct((nb, s, N_HEADS * HEAD_PAD), BF16),
                 jax.ShapeDtypeStruct((nb, s, N_HEADS * HEAD_PAD), BF16),
                 jax.ShapeDtypeStruct((nb, s, KV_LORA_RANK), BF16)]
    out_specs = [tok((1, tile, CACHE_ROW)), tok((1, tile, N_HEADS * HEAD_PAD)),
                 tok((1, tile, N_HEADS * HEAD_PAD)), tok((1, tile, KV_LORA_RANK))]
    if absorb:
        out_shape += [jax.ShapeDtypeStruct((nb, s, N_HEADS * KV_LORA_RANK), BF16),
                      jax.ShapeDtypeStruct((nb, s, 128), F32)]
        out_specs += [tok((1, tile, N_HEADS * KV_LORA_RANK)), tok((1, tile, 128))]
    return pl.pallas_call(
        functools.partial(_mla_proj_kernel, absorb),
        grid=(nb, ns),
        in_specs=[tok((1, tile, D_MODEL)), mod, mod, full(gmix),
                  pl.BlockSpec((tile, ROPE_PAD), lambda b, i: (i, 0)),
                  pl.BlockSpec((tile, ROPE_PAD), lambda b, i: (i, 0))] + [full(w) for w in weights],
        out_specs=out_specs,
        out_shape=out_shape,
        compiler_params=_cparams(2),
        name="mla_project_sample" if absorb else "mla_project",
    )(x, shift, scale, gmix, cs, sn, *weights)


def _attn_out(ctx_of_head, x, gate, w_uv_ref, w_out_ref):
    o = jnp.concatenate([_bdot(ctx_of_head(hd), w_uv_ref[hd]) for hd in range(N_HEADS)], axis=-1)
    return x + gate * _bdot(o, w_out_ref[...])


def _flash_kernel(q_ref, k_ref, v_ref, x_ref, gate_ref, w_uv_ref, w_out_ref, o_ref,
                  m_s, l_s, acc_s):
    qi = pl.program_id(1)
    ki = pl.program_id(2)
    tq = q_ref.shape[1]

    @pl.when(ki == 0)
    def _():
        m_s[...] = jnp.full(m_s.shape, NEG_INF, F32)
        l_s[...] = jnp.zeros(l_s.shape, F32)
        acc_s[...] = jnp.zeros(acc_s.shape, F32)

    def step(masked):
        v = v_ref[0]
        q_sub = min(Q_SUB, tq)
        for qs in range(tq // q_sub):
            rows = slice(qs * q_sub, (qs + 1) * q_sub)
            if masked:
                row = lax.broadcasted_iota(jnp.int32, (q_sub, tq), 0) + qs * q_sub
                col = lax.broadcasted_iota(jnp.int32, (q_sub, tq), 1)
                keep = col <= row
            for hd in range(N_HEADS):
                o = hd * HEAD_PAD
                s = _bdot_nt(q_ref[0, rows, o:o + HEAD_PAD], k_ref[0, :, o:o + HEAD_PAD])
                if masked:
                    s = jnp.where(keep, s, NEG_INF)
                m_prev = m_s[hd, rows]
                m_new = jnp.maximum(m_prev, jnp.max(s, axis=-1, keepdims=True))
                alpha = jnp.exp(m_prev - m_new)
                p = jnp.exp(s - m_new)
                l_s[hd, rows] = alpha * l_s[hd, rows] + jnp.sum(p, axis=-1, keepdims=True)
                acc_s[hd, rows] = alpha * acc_s[hd, rows] + _bdot(p, v)
                m_s[hd, rows] = m_new

    @pl.when(ki < qi)
    def _():
        step(False)

    @pl.when(ki == qi)
    def _():
        step(True)
        o_ref[0] = _attn_out(lambda hd: acc_s[hd] / l_s[hd], x_ref[0], gate_ref[0],
                             w_uv_ref, w_out_ref)


def _prompt_attention(q, k, v, x, gate, w_uv, w_out, tile):
    nb, s, _ = x.shape
    nt = s // tile
    full = lambda a: pl.BlockSpec(a.shape, lambda b, i, j: (0,) * a.ndim)
    return pl.pallas_call(
        _flash_kernel,
        grid=(nb, nt, nt),
        in_specs=[
            pl.BlockSpec((1, tile, N_HEADS * HEAD_PAD), lambda b, i, j: (b, i, 0)),
            pl.BlockSpec((1, tile, N_HEADS * HEAD_PAD), lambda b, i, j: (b, jnp.minimum(i, j), 0)),
            pl.BlockSpec((1, tile, KV_LORA_RANK), lambda b, i, j: (b, jnp.minimum(i, j), 0)),
            pl.BlockSpec((1, tile, D_MODEL), lambda b, i, j: (b, i, 0)),
            pl.BlockSpec((1, 1, D_MODEL), lambda b, i, j: (b, 0, 0)),
            full(w_uv), full(w_out),
        ],
        out_specs=pl.BlockSpec((1, tile, D_MODEL), lambda b, i, j: (b, i, 0)),
        out_shape=jax.ShapeDtypeStruct(x.shape, F32),
        scratch_shapes=[pltpu.VMEM((N_HEADS, tile, 1), F32), pltpu.VMEM((N_HEADS, tile, 1), F32),
                        pltpu.VMEM((N_HEADS, tile, KV_LORA_RANK), F32)],
        compiler_params=_cparams(3),
        name="prompt_attention",
    )(q, k, v, x, gate, w_uv, w_out)


PAGES_PER_STEP = 32
PAGES_PER_DOT = 4
QA_ROWS = 16


def _sample_attn_kernel(pt_ref, qa_ref, qr_ref, sself_ref, lat_new_ref, w_ukt_ref, *rest):
    del pt_ref
    pages = rest[:PAGES_PER_STEP]
    ctx_ref, m_s, l_s, acc_s, lat_s = rest[PAGES_PER_STEP:]
    step = pl.program_id(1)

    @pl.when(step == 0)
    def _():
        m_s[...] = sself_ref[0]
        l_s[...] = jnp.ones(l_s.shape, F32)
        acc_s[...] = jnp.broadcast_to(lat_new_ref[0], acc_s.shape)

    n_key_rows = N_HEADS * QK_NOPE_DIM
    n_dot = PAGES_PER_DOT * PAGE_SIZE
    lhs = jnp.concatenate([w_ukt_ref[...], qa_ref[0]], axis=0)
    qr = qr_ref[0]
    head_row = lax.broadcasted_iota(jnp.int32, (N_HEADS, n_dot), 0)
    scores = []
    for c in range(PAGES_PER_STEP // PAGES_PER_DOT):
        rows_t = jnp.concatenate([pages[c * PAGES_PER_DOT + p][...] for p in range(PAGES_PER_DOT)], axis=1)
        lat_t = rows_t[:KV_LORA_RANK].astype(BF16)
        kr_t = rows_t[KV_LORA_RANK:].astype(BF16)
        lat_s[:, c * n_dot:(c + 1) * n_dot] = lat_t
        yq = jnp.dot(lhs, lat_t, preferred_element_type=F32)
        ssq = jnp.zeros(head_row.shape, F32)
        for hd in range(N_HEADS):
            y = yq[hd * QK_NOPE_DIM:(hd + 1) * QK_NOPE_DIM]
            ssq = jnp.where(head_row == hd, jnp.sum(y * y, axis=0, keepdims=True), ssq)
        rinv = lax.rsqrt(ssq * (1.0 / QK_NOPE_DIM) + EPS)
        scores.append(yq[n_key_rows:n_key_rows + N_HEADS] * rinv
                      + jnp.dot(qr, kr_t, preferred_element_type=F32))
    s = jnp.concatenate(scores, axis=1)
    m_prev = m_s[...]
    m_new = jnp.maximum(m_prev, jnp.max(s, axis=-1, keepdims=True))
    alpha = jnp.exp(m_prev - m_new)
    p = jnp.exp(s - m_new)
    l_s[...] = alpha * l_s[...] + jnp.sum(p, axis=-1, keepdims=True)
    acc_s[...] = alpha * acc_s[...] + _bdot_nt(p, lat_s[...])
    m_s[...] = m_new

    @pl.when(step == pl.num_programs(1) - 1)
    def _():
        ctx_ref[0] = acc_s[...] / l_s[...]


def _sample_attention(cache_t, layer, page_table, qa, qr, sself, lat_new, w_ukt):
    nb, n_pages = page_table.shape
    n_steps = n_pages // PAGES_PER_STEP
    per_b = lambda a: pl.BlockSpec((1,) + a.shape[1:], lambda b, s, pt: (b,) + (0,) * (a.ndim - 1))
    full = lambda a: pl.BlockSpec(a.shape, lambda b, s, pt: (0,) * a.ndim)

    def page_spec(j):
        return pl.BlockSpec((None, None, CACHE_ROW, PAGE_SIZE),
                            lambda b, s, pt: (layer, pt[b * n_pages + s * PAGES_PER_STEP + j], 0, 0))

    grid_spec = pltpu.PrefetchScalarGridSpec(
        num_scalar_prefetch=1,
        grid=(nb, n_steps),
        in_specs=[per_b(qa), per_b(qr), per_b(sself), per_b(lat_new), full(w_ukt)]
        + [page_spec(j) for j in range(PAGES_PER_STEP)],
        out_specs=pl.BlockSpec((1, N_HEADS, KV_LORA_RANK), lambda b, s, pt: (b, 0, 0)),
        scratch_shapes=[pltpu.VMEM((N_HEADS, 1), F32), pltpu.VMEM((N_HEADS, 1), F32),
                        pltpu.VMEM((N_HEADS, KV_LORA_RANK), F32),
                        pltpu.VMEM((KV_LORA_RANK, PAGES_PER_STEP * PAGE_SIZE), BF16)],
    )
    return pl.pallas_call(
        _sample_attn_kernel,
        grid_spec=grid_spec,
        out_shape=jax.ShapeDtypeStruct((nb, N_HEADS, KV_LORA_RANK), F32),
        compiler_params=_cparams(2),
        name="sample_attention",
    )(page_table.reshape(-1), qa, qr, sself, lat_new, w_ukt, *([cache_t] * PAGES_PER_STEP))


def _sample_out_kernel(ctx_ref, x_ref, gate_ref, w_uv_ref, w_out_ref, o_ref):
    o_ref[...] = _attn_out(lambda hd: ctx_ref[:, hd * KV_LORA_RANK:(hd + 1) * KV_LORA_RANK],
                           x_ref[...], gate_ref[...], w_uv_ref, w_out_ref)


def _sample_out(ctx, x, gate, w_uv, w_out):
    return pl.pallas_call(
        _sample_out_kernel,
        out_shape=jax.ShapeDtypeStruct(x.shape, F32),
        compiler_params=pltpu.CompilerParams(vmem_limit_bytes=VMEM_LIMIT),
        name="sample_attention_out",
    )(ctx, x, gate, w_uv, w_out)


def _pool_mix(window_sum, h, cnt, w_pool_ref, pscale):
    parts = []
    for gi, w in enumerate(POOL_WINDOWS):
        ch = slice(gi * POOL_GROUP_DIM, (gi + 1) * POOL_GROUP_DIM)
        acc = h[:, ch]
        for kk in range(1, w):
            acc = acc + window_sum(kk, ch)
        d = acc / cnt[gi] - h[:, ch]
        parts.append(_bdot(d, w_pool_ref[gi]))
    return jnp.concatenate(parts, axis=-1) * pscale


def _pool_prompt_kernel(x_ref, shift_ref, scale_ref, gate_ref, gmix_ref, w_pool_ref, pscale_ref,
                        o_ref, st_ref, ext_s):
    si = pl.program_id(1)
    t = x_ref.shape[1]

    @pl.when(si == 0)
    def _():
        ext_s[0:POOL_HALO] = jnp.zeros((POOL_HALO, D_MODEL), F32)

    @pl.when(si > 0)
    def _():
        ext_s[0:POOL_HALO] = ext_s[t:t + POOL_HALO]

    x = x_ref[0]
    h = _rms(x, gmix_ref[...]) * (1.0 + scale_ref[0]) + shift_ref[0]
    ext_s[POOL_HALO:POOL_HALO + t] = h
    pos = si * t + lax.broadcasted_iota(jnp.int32, (t, 1), 0)
    cnt = [jnp.minimum(pos + 1, w).astype(F32) for w in POOL_WINDOWS]
    y = _pool_mix(lambda kk, ch: ext_s[POOL_HALO - kk:POOL_HALO - kk + t, ch], h, cnt,
                  w_pool_ref, pscale_ref[...])
    o_ref[0] = x + gate_ref[0] * y

    @pl.when(si == pl.num_programs(1) - 1)
    def _():
        st_ref[0] = ext_s[POOL_HALO + t - POOL_STATE_LEN:POOL_HALO + t]


def _pool_prompt(x, shift, scale, gate, gmix, w_pool, pscale, tile):
    nb, s, _ = x.shape
    tok = pl.BlockSpec((1, tile, D_MODEL), lambda b, i: (b, i, 0))
    mod = pl.BlockSpec((1, 1, D_MODEL), lambda b, i: (b, 0, 0))
    full = lambda a: pl.BlockSpec(a.shape, lambda b, i: (0,) * a.ndim)
    return pl.pallas_call(
        _pool_prompt_kernel,
        grid=(nb, s // tile),
        in_specs=[tok, mod, mod, mod, full(gmix), full(w_pool), full(pscale)],
        out_specs=[tok, pl.BlockSpec((1, POOL_STATE_LEN, D_MODEL), lambda b, i: (b, 0, 0))],
        out_shape=[jax.ShapeDtypeStruct(x.shape, F32),
                   jax.ShapeDtypeStruct((nb, POOL_STATE_LEN, D_MODEL), F32)],
        scratch_shapes=[pltpu.VMEM((POOL_HALO + tile, D_MODEL), F32)],
        compiler_params=_cparams(2),
        name="pool_prompt",
    )(x, shift, scale, gate, gmix, w_pool, pscale)


def _pool_sample_kernel(x_ref, shift_ref, scale_ref, gate_ref, gmix_ref, w_pool_ref, pscale_ref,
                        state_ref, o_ref, st_ref):
    x = x_ref[...]
    h = _rms(x, gmix_ref[...]) * (1.0 + scale_ref[...]) + shift_ref[...]
    cnt = [float(w) for w in POOL_WINDOWS]
    y = _pool_mix(lambda kk, ch: state_ref[POOL_STATE_LEN - kk, :, ch], h, cnt,
                  w_pool_ref, pscale_ref[...])
    o_ref[...] = x + gate_ref[...] * y
    for r in range(POOL_STATE_LEN - 1):
        st_ref[r] = state_ref[r + 1]
    st_ref[POOL_STATE_LEN - 1] = h


def _pool_sample(x, shift, scale, gate, gmix, w_pool, pscale, state_t):
    return pl.pallas_call(
        _pool_sample_kernel,
        out_shape=[jax.ShapeDtypeStruct(x.shape, F32), jax.ShapeDtypeStruct(state_t.shape, F32)],
        compiler_params=pltpu.CompilerParams(vmem_limit_bytes=VMEM_LIMIT),
        name="pool_sample",
    )(x, shift, scale, gate, gmix, w_pool, pscale, state_t)


def _cand_rows(b):
    return PEER_TOPK // (b + 1)


NOT_TOP = 64.0
LANES = 128
SUBLANES = 8


def _sort_pairs(n):
    pairs, p = [], 1
    while p < n:
        k = p
        while k >= 1:
            for j in range(k % p, n - k, 2 * k):
                for i in range(min(k, n - j - k)):
                    if (i + j) // (2 * p) == (i + j + k) // (2 * p):
                        pairs.append((i + j, i + j + k))
            k //= 2
        p *= 2
    return pairs


def _bitonic_merge_pairs(n):
    pairs, k = [], n // 2
    while k >= 1:
        pairs += [(i, i + k) for i in range(n) if not i & k]
        k //= 2
    return pairs


def _compare_exchange(v, pairs):
    for i, j in pairs:
        v[i], v[j] = jnp.maximum(v[i], v[j]), jnp.minimum(v[i], v[j])
    return v


def _top16_sorted(s):
    v = [s[r * SUBLANES:(r + 1) * SUBLANES] for r in range(PEER_TOPK)]
    v = _compare_exchange(v, _sort_pairs(PEER_TOPK))
    for shift in (4, 2, 1):
        v = [jnp.maximum(v[i], pltpu.roll(v[PEER_TOPK - 1 - i], shift, axis=0)) for i in range(PEER_TOPK)]
        v = _compare_exchange(v, _bitonic_merge_pairs(PEER_TOPK))
    return v


def _kth_largest(work, k):
    for _ in range(k):
        m = jnp.max(work, axis=0, keepdims=True)
        work = jnp.where(work == m, NEG_INF, work)
    return m


def _prefix_count(x, thresholds, above):
    parts = []
    for r in range(x.shape[0] // SUBLANES):
        xr = x[r * SUBLANES:(r + 1) * SUBLANES]
        cnt = jnp.zeros(xr.shape, F32)
        for b, th in enumerate(thresholds):
            cnt = jnp.where(th > xr if above else xr >= th, float(b + 1), cnt)
        parts.append(cnt)
    return jnp.concatenate(parts, axis=0)


def _dup_bf16(x):
    hi = pltpu.bitcast(x.astype(BF16).astype(F32), jnp.uint32)
    return hi | (hi >> 16)


def _pack_pairs(x):
    return pltpu.bitcast(x.astype(BF16), jnp.uint32)


def _peer_route_kernel(x_ref, shift_ref, scale_ref, gffn_ref, wqt_ref, subk_ref,
                       ht_ref, r2_ref, e2_ref, n1_ref, e1_ref):
    t = x_ref.shape[1]
    h = _rms(x_ref[0], gffn_ref[...]) * (1.0 + scale_ref[0]) + shift_ref[0]
    ht = h.T.astype(BF16)
    ht_ref[...] = pltpu.bitcast(ht, jnp.uint32)
    qt = jnp.dot(wqt_ref[...], ht, preferred_element_type=F32).astype(BF16)
    row8 = lax.broadcasted_iota(jnp.int32, (SUBLANES, LANES), 0)
    for hd in range(PEER_HEADS):
        sc = [jnp.dot(subk_ref[2 * hd + p], qt[(2 * hd + p) * PEER_HALF:(2 * hd + p + 1) * PEER_HALF],
                      preferred_element_type=F32) for p in range(2)]
        for lt in range(t // LANES):
            ln = slice(lt * LANES, (lt + 1) * LANES)
            s1, s2 = sc[0][:, ln], sc[1][:, ln]
            top1, top2 = _top16_sorted(s1), _top16_sorted(s2)
            v1_lo, v1_hi = top1[0], top1[SUBLANES]
            for a in range(1, SUBLANES):
                v1_lo = jnp.where(row8 == a, top1[a], v1_lo)
                v1_hi = jnp.where(row8 == a, top1[SUBLANES + a], v1_hi)
            cand = [v1_lo + top2[0], v1_hi + top2[0]]
            for b in range(1, PEER_TOPK):
                cand.append(jnp.where(row8 < _cand_rows(b), v1_lo + top2[b], NEG_INF))
            cand = jnp.concatenate(cand, axis=0)
            tau = _kth_largest(cand, PEER_TOPK)
            cmax = top1[0] + top2[0]
            z = jnp.sum(jnp.where(cand >= tau, jnp.exp(cand - cmax[:1]), 0.0), axis=0, keepdims=True)
            theta = []
            for b in range(PEER_TOPK):
                lo = jnp.where(v1_lo + top2[b] >= tau, v1_lo, -NEG_INF)
                hi = jnp.where(v1_hi + top2[b] >= tau, v1_hi, -NEG_INF)
                theta.append(jnp.min(jnp.minimum(lo, hi), axis=0, keepdims=True))
            n1 = _prefix_count(s1, theta, above=False)
            rank2 = _prefix_count(s2, top2, above=True)
            rank2 = jnp.where(rank2 >= PEER_TOPK, NOT_TOP, rank2)
            r2_ref[hd, :, ln] = _pack_pairs(rank2)
            e2_ref[hd, :, ln] = _pack_pairs(jnp.exp(s2 - top2[0][:1]))
            n1_ref[hd, lt] = _dup_bf16(n1)
            e1_ref[hd, lt] = _dup_bf16(jnp.exp(s1 - top1[0][:1]) * (0.5 / z))


def _peer_route(x, shift, scale, gffn, wqt, subk, tile):
    nb, s, _ = x.shape
    r = shift.shape[1]
    ns = s // tile
    n = nb * s
    rt = tile if r == s else 1
    tok = pl.BlockSpec((1, tile, D_MODEL), lambda b, i: (b, i, 0))
    mod = pl.BlockSpec((1, rt, D_MODEL), (lambda b, i: (b, i, 0)) if r == s else (lambda b, i: (b, 0, 0)))
    full = lambda a: pl.BlockSpec(a.shape, lambda b, i: (0,) * a.ndim)
    u32 = jnp.uint32
    lane_tiles = tile // LANES
    pairs = pl.BlockSpec((PEER_HEADS, PEER_N_KEYS // 2, tile), lambda b, i: (0, 0, b * ns + i))
    rows = pl.BlockSpec((PEER_HEADS, lane_tiles, PEER_N_KEYS, LANES), lambda b, i: (0, b * ns + i, 0, 0))
    pairs_shape = jax.ShapeDtypeStruct((PEER_HEADS, PEER_N_KEYS // 2, n), u32)
    rows_shape = jax.ShapeDtypeStruct((PEER_HEADS, n // LANES, PEER_N_KEYS, LANES), u32)
    return pl.pallas_call(
        _peer_route_kernel,
        grid=(nb, ns),
        in_specs=[tok, mod, mod, full(gffn), full(wqt), full(subk)],
        out_specs=[pl.BlockSpec((D_MODEL // 2, tile), lambda b, i: (0, b * ns + i)),
                   pairs, pairs, rows, rows],
        out_shape=[jax.ShapeDtypeStruct((D_MODEL // 2, n), u32), pairs_shape, pairs_shape,
                   rows_shape, rows_shape],
        compiler_params=_cparams(2),
        name="peer_route",
    )(x, shift, scale, gffn, wqt, subk)


N_CHUNKS = PEER_N_EXPERTS // EXPERT_CHUNK
GATE_ROWS = 32
OUT_DEPTH = 512
OUT_ROWS = 512
KEYS_PER_PIECE = 2


def _peer_sweep_kernel(ht_ref, r2_ref, e2_ref, n1_ref, e1_ref, u_ref, vt_ref, x_ref, gate_ref, o_ref,
                       ga_s, yt_s):
    j = pl.program_id(2)
    t = ht_ref.shape[1]

    @pl.when(j == 0)
    def _():
        yt_s[...] = jnp.zeros(yt_s.shape, F32)

    ht = pltpu.bitcast(ht_ref[...], BF16)
    zero = jnp.zeros((GATE_ROWS, LANES), BF16)
    n_groups = PEER_N_KEYS // GATE_ROWS

    def gate_piece(k):
        n_rows = KEYS_PER_PIECE * PEER_N_KEYS
        u = pltpu.bitcast(u_ref[k * n_rows // 2:(k + 1) * n_rows // 2, :], BF16)
        act = jnp.dot(u, ht, preferred_element_type=F32)
        ga_s[k * n_rows:(k + 1) * n_rows, :] = (act * (1.0 + lax.erf(act * (2.0 ** -0.5)))).astype(BF16)
        for lt in range(t // LANES):
            ln = slice(lt * LANES, (lt + 1) * LANES)
            g = [[zero] * n_groups for _ in range(KEYS_PER_PIECE)]
            for hd in range(PEER_HEADS):
                n1, e1 = [], []
                for kk in range(KEYS_PER_PIECE):
                    row = pl.ds(j * KEY_ROWS_PER_CHUNK + k * KEYS_PER_PIECE + kk, 1)
                    spread = lambda ref: pltpu.bitcast(
                        jnp.broadcast_to(ref[hd, lt, row, :], (GATE_ROWS // 2, LANES)), BF16)
                    n1.append(spread(n1_ref))
                    e1.append(spread(e1_ref))
                for sg in range(n_groups):
                    pairs = slice(sg * GATE_ROWS // 2, (sg + 1) * GATE_ROWS // 2)
                    r2 = pltpu.bitcast(r2_ref[hd, pairs, ln], BF16)
                    e2 = pltpu.bitcast(e2_ref[hd, pairs, ln], BF16)
                    for kk in range(KEYS_PER_PIECE):
                        g[kk][sg] = g[kk][sg] + jnp.where(r2 < n1[kk], e2 * e1[kk], zero)
            for kk in range(KEYS_PER_PIECE):
                for sg in range(n_groups):
                    r0 = k * n_rows + kk * PEER_N_KEYS + sg * GATE_ROWS
                    ga_s[r0:r0 + GATE_ROWS, ln] = ga_s[r0:r0 + GATE_ROWS, ln] * g[kk][sg]

    def out_piece(group, m):
        cols = slice(group * OUT_DEPTH, (group + 1) * OUT_DEPTH)
        rows = slice(m * OUT_ROWS, (m + 1) * OUT_ROWS)
        v = pltpu.bitcast(vt_ref[m * OUT_ROWS // 2:(m + 1) * OUT_ROWS // 2, cols], BF16)
        yt_s[rows, :] += jnp.dot(v, ga_s[cols, :], preferred_element_type=F32)

    pieces_per_group = OUT_DEPTH // (KEYS_PER_PIECE * PEER_N_KEYS)
    n_pieces = KEY_ROWS_PER_CHUNK // KEYS_PER_PIECE
    pending = []
    for k in range(n_pieces):
        gate_piece(k)
        if pending:
            out_piece(*pending.pop(0))
        if (k + 1) % pieces_per_group == 0:
            pending += [(k // pieces_per_group, m) for m in range(D_MODEL // OUT_ROWS)]
    for piece in pending:
        out_piece(*piece)

    @pl.when(j == pl.num_programs(2) - 1)
    def _():
        o_ref[0] = x_ref[0] + gate_ref[0] * yt_s[...].T


def _peer_sweep(ht, r2, e2, n1, e1, u, vt, x, gate, tile):
    nb, s, _ = x.shape
    r = gate.shape[1]
    ns = s // tile
    pairs = pl.BlockSpec((PEER_HEADS, PEER_N_KEYS // 2, tile), lambda b, i, j: (0, 0, b * ns + i))
    rows = pl.BlockSpec((PEER_HEADS, tile // LANES, PEER_N_KEYS, LANES),
                        lambda b, i, j: (0, b * ns + i, 0, 0))
    tok = pl.BlockSpec((1, tile, D_MODEL), lambda b, i, j: (b, i, 0))
    mod = pl.BlockSpec((1, tile if r == s else 1, D_MODEL),
                       (lambda b, i, j: (b, i, 0)) if r == s else (lambda b, i, j: (b, 0, 0)))
    return pl.pallas_call(
        _peer_sweep_kernel,
        grid=(nb, ns, N_CHUNKS),
        in_specs=[pl.BlockSpec((D_MODEL // 2, tile), lambda b, i, j: (0, b * ns + i)),
                  pairs, pairs, rows, rows,
                  pl.BlockSpec((EXPERT_CHUNK // 2, D_MODEL), lambda b, i, j: (j, 0)),
                  pl.BlockSpec((D_MODEL // 2, EXPERT_CHUNK), lambda b, i, j: (0, j)),
                  tok, mod],
        out_specs=tok,
        out_shape=jax.ShapeDtypeStruct(x.shape, F32),
        scratch_shapes=[pltpu.VMEM((EXPERT_CHUNK, tile), BF16), pltpu.VMEM((D_MODEL, tile), F32)],
        compiler_params=_cparams(3),
        name="peer_sweep",
    )(ht, r2, e2, n1, e1, u, vt, x, gate)


def _peer(x, shift, scale, gate, gffn, pw, route_tile, sweep_tile):
    routed = _peer_route(x, shift, scale, gffn, pw["wqt"], pw["subk"], route_tile)
    return _peer_sweep(*routed, pw["u"], pw["vt"], x, gate, sweep_tile)


def _rope_tables(pos):
    inv = ROPE_THETA ** (-jnp.arange(0, QK_ROPE_DIM, 2, dtype=F32) / QK_ROPE_DIM)
    ang = pos.astype(F32)[:, None] * inv[None, :]
    c, s, z = jnp.cos(ang), jnp.sin(ang), jnp.zeros_like(ang)
    return jnp.concatenate([c, z, c, z], -1), jnp.concatenate([-s, z, s, z], -1)


def _pad_rope_cols(w):
    z = jnp.zeros(w.shape[:-1] + (ROPE_HALF,), w.dtype)
    return jnp.concatenate([w[..., :ROPE_HALF], z, w[..., ROPE_HALF:], z], -1)


def _mla_weights(w_in, g_q_lat, g_kv_lat, w_uq, w_uk, w_uv, g_qn_nope, g_qn_rope, g_kn_nope,
                 g_kn_rope, w_out):
    lat_end = Q_LORA_RANK + KV_LORA_RANK
    w_in_p = jnp.concatenate([w_in[:, :lat_end], _pad_rope_cols(w_in[:, lat_end:])], -1)
    wq = w_uq.reshape(Q_LORA_RANK, N_HEADS, QK_NOPE_DIM + QK_ROPE_DIM)
    wq = jnp.concatenate([wq[..., :QK_NOPE_DIM], _pad_rope_cols(wq[..., QK_NOPE_DIM:])], -1)
    row = lambda g: g.reshape(1, -1)
    return {
        "w_in": w_in_p.astype(BF16),
        "g_q_lat": row(g_q_lat), "g_kv_lat": row(g_kv_lat), "g_kr": row(_pad_rope_cols(g_kn_rope)),
        "w_uq": wq.reshape(Q_LORA_RANK, N_HEADS * HEAD_PAD).astype(BF16),
        "g_qn": row(g_qn_nope), "g_qr": row(_pad_rope_cols(g_qn_rope)),
        "w_uk": w_uk.reshape(KV_LORA_RANK, N_HEADS * QK_NOPE_DIM).astype(BF16),
        "g_kn": row(g_kn_nope),
        "w_ukt": jnp.transpose(w_uk, (1, 2, 0)).astype(BF16),
        "w_uv": jnp.transpose(w_uv, (1, 0, 2)).astype(BF16),
        "w_out": w_out.astype(BF16),
    }


def _pack_row_pairs(w):
    r, c = w.shape
    return lax.bitcast_convert_type(jnp.transpose(w.reshape(r // 2, 2, c), (0, 2, 1)), jnp.uint32)


def _peer_weights(w_q, subkeys, u, v):
    return {
        "wqt": w_q.T.astype(BF16),
        "subk": subkeys.reshape(2 * PEER_HEADS, PEER_N_KEYS, PEER_HALF).astype(BF16),
        "u": _pack_row_pairs(u.astype(BF16)),
        "vt": _pack_row_pairs(v.T.astype(BF16)),
    }


PROMPT_TILE = 256
SAMPLE_TILE = 128
PEER_SWEEP_TILE = 512
ATTN_TILE = 512
Q_SUB = 256


def kernel(x_prompt, x_sample, c_prompt, c_sample, cache_mla, state_pool, page_table, norm_mix, norm_ffn, w_ada, b_ada, w_mla_in, g_q_lat, g_kv_lat, w_uq, w_uk, w_uv, g_qn_nope, g_qn_rope, g_kn_nope, g_kn_rope, w_mla_out, w_pool, pool_scale, w_peer_q, peer_subkeys, peer_u, peer_v):
    nbp, seq, _ = x_prompt.shape
    nbs = x_sample.shape[0]
    depth = w_ada.shape[0]
    xp = x_prompt
    xs = x_sample.reshape(1, nbs, D_MODEL)

    mods = _ada_params(jnp.concatenate([c_prompt, c_sample], 0), w_ada, b_ada)
    mods = mods.reshape(depth, nbp + nbs, 6, D_MODEL)
    cs_p, sn_p = _rope_tables(jnp.arange(seq))
    cs_s, sn_s = _rope_tables(jnp.full((nbs,), PAST_LEN))
    cache_t = jnp.swapaxes(cache_mla, 2, 3)

    rows_p, rows_s, pst_p, pst_s = [], [], [], []
    for l in range(depth):
        mp = [mods[l, :nbp, k].reshape(nbp, 1, D_MODEL) for k in range(6)]
        ms = [mods[l, nbp:, k].reshape(1, nbs, D_MODEL) for k in range(6)]
        gmix = norm_mix[l].reshape(1, D_MODEL)
        gffn = norm_ffn[l].reshape(1, D_MODEL)
        i = l // 2
        if l % 2 == 0:
            mw = _mla_weights(w_mla_in[i], g_q_lat[i], g_kv_lat[i], w_uq[i], w_uk[i], w_uv[i],
                              g_qn_nope[i], g_qn_rope[i], g_kn_nope[i], g_kn_rope[i], w_mla_out[i])
            r_p, q, k, v = _mla_project(xp, mp[0], mp[1], gmix, cs_p, sn_p, mw, PROMPT_TILE, False)
            xp = _prompt_attention(q, k, v, xp, mp[2], mw["w_uv"], mw["w_out"], ATTN_TILE)
            r_s, q, k, v, qa, sself = _mla_project(xs, ms[0], ms[1], gmix, cs_s, sn_s, mw,
                                                   SAMPLE_TILE, True)
            q = q.reshape(nbs, N_HEADS, HEAD_PAD)[:, :, QK_NOPE_DIM:]
            qr = jnp.concatenate([q[..., :ROPE_HALF], q[..., 2 * ROPE_HALF:3 * ROPE_HALF]], -1)
            qa = jnp.pad(qa.reshape(nbs, N_HEADS, KV_LORA_RANK), ((0, 0), (0, QA_ROWS - N_HEADS), (0, 0)))
            ctx = _sample_attention(
                cache_t, i, page_table, qa, qr, sself.reshape(nbs, 128)[:, :N_HEADS, None],
                r_s.reshape(nbs, 1, CACHE_ROW)[..., :KV_LORA_RANK],
                mw["w_ukt"].reshape(N_HEADS * QK_NOPE_DIM, KV_LORA_RANK))
            xs = _sample_out(ctx.reshape(nbs, N_HEADS * KV_LORA_RANK), xs[0], ms[2][0],
                             mw["w_uv"], mw["w_out"]).reshape(1, nbs, D_MODEL)
            rows_p.append(r_p)
            rows_s.append(r_s.reshape(nbs, 1, CACHE_ROW))
        else:
            wp = w_pool[i].astype(BF16)
            psc = pool_scale[i].reshape(1, D_MODEL)
            xp, st_p = _pool_prompt(xp, mp[0], mp[1], mp[2], gmix, wp, psc, PROMPT_TILE)
            x2, st_s = _pool_sample(xs[0], ms[0][0], ms[1][0], ms[2][0], gmix, wp, psc,
                                    jnp.transpose(state_pool[i], (1, 0, 2)))
            xs = x2.reshape(1, nbs, D_MODEL)
            pst_p.append(st_p)
            pst_s.append(jnp.transpose(st_s, (1, 0, 2)))
        pw = _peer_weights(w_peer_q[l], peer_subkeys[l], peer_u[l], peer_v[l])
        xp = _peer(xp, mp[3], mp[4], mp[5], gffn, pw, PROMPT_TILE, PEER_SWEEP_TILE)
        xs = _peer(xs, ms[3], ms[4], ms[5], gffn, pw, SAMPLE_TILE, SAMPLE_TILE)
    return (xp, xs.reshape(nbs, 1, D_MODEL), jnp.stack(rows_p), jnp.stack(rows_s),
            jnp.stack(pst_p), jnp.stack(pst_s))
```

```python
import functools

import jax
import jax.numpy as jnp
from jax import lax
from jax.experimental import pallas as pl
from jax.experimental.pallas import tpu as pltpu

F32 = jnp.float32
BF16 = jnp.bfloat16

D_MODEL = 1024
EPS = 1e-6
PAST_LEN = 8192
PAGE_SIZE = 128
N_HEADS = 8
QK_NOPE_DIM = 128
QK_ROPE_DIM = 64
ROPE_HALF = QK_ROPE_DIM // 2
ROPE_PAD = 128
HEAD_PAD = QK_NOPE_DIM + ROPE_PAD
V_HEAD_DIM = 128
Q_LORA_RANK = 384
KV_LORA_RANK = 256
CACHE_ROW = KV_LORA_RANK + QK_ROPE_DIM
ROPE_THETA = 10000.0
ATTN_SCALE = (QK_NOPE_DIM + QK_ROPE_DIM) ** -0.5
POOL_WINDOWS = (2, 4, 8, 16)
POOL_GROUP_DIM = D_MODEL // len(POOL_WINDOWS)
POOL_STATE_LEN = max(POOL_WINDOWS) - 1
POOL_HALO = 16
PEER_HEADS = 8
PEER_N_KEYS = 128
PEER_N_EXPERTS = PEER_N_KEYS * PEER_N_KEYS
PEER_TOPK = 16
PEER_HALF = 128
EXPERT_CHUNK = 2048
KEY_ROWS_PER_CHUNK = EXPERT_CHUNK // PEER_N_KEYS

VMEM_LIMIT = 56 * 1024 * 1024

NEG_INF = float("-inf")


def _cparams(n_grid):
    return pltpu.CompilerParams(
        dimension_semantics=("arbitrary",) * n_grid, vmem_limit_bytes=VMEM_LIMIT)


def _rms(x, g, n=None):
    n = x.shape[-1] if n is None else n
    ms = jnp.sum(x * x, axis=-1, keepdims=True) * (1.0 / n)
    return x * lax.rsqrt(ms + EPS) * g


def _bdot(a, b):
    return jnp.dot(a.astype(BF16), b.astype(BF16), preferred_element_type=F32)


def _bdot_nt(a, b):
    return lax.dot_general(a.astype(BF16), b.astype(BF16), (((1,), (1,)), ((), ())),
                           preferred_element_type=F32)


def _mods_kernel(c_ref, w_ref, b_ref, o_ref):
    o_ref[0] = _bdot(jax.nn.silu(c_ref[...]), w_ref[0]) + b_ref[0]


def _ada_params(c_all, w_ada, b_ada):
    n_layers = w_ada.shape[0]
    n_rows = c_all.shape[0]
    n_col = w_ada.shape[2] // D_MODEL
    return pl.pallas_call(
        _mods_kernel,
        grid=(n_layers, n_col),
        in_specs=[
            pl.BlockSpec((n_rows, D_MODEL), lambda l, k: (0, 0)),
            pl.BlockSpec((1, D_MODEL, D_MODEL), lambda l, k: (l, 0, k)),
            pl.BlockSpec((1, 1, D_MODEL), lambda l, k: (l, 0, k)),
        ],
        out_specs=pl.BlockSpec((1, n_rows, D_MODEL), lambda l, k: (l, 0, k)),
        out_shape=jax.ShapeDtypeStruct((n_layers, n_rows, n_col * D_MODEL), F32),
        compiler_params=_cparams(2),
        name="ada_params",
    )(c_all, w_ada, b_ada.reshape(n_layers, 1, -1))


def _rope_pad(x, g, cs, sn):
    y = _rms(x, g, QK_ROPE_DIM)
    return y * cs + pltpu.roll(y, ROPE_PAD // 2, axis=1) * sn


def _mla_proj_kernel(absorb, x_ref, shift_ref, scale_ref, gmix_ref, cs_ref, sn_ref, w_in_ref,
                     g_q_lat_ref, g_kv_lat_ref, g_kr_ref, w_uq_ref, g_qn_ref, g_qr_ref,
                     w_uk_ref, g_kn_ref, *rest):
    if absorb:
        w_ukt_ref, rows_ref, q_ref, k_ref, v_ref, qa_ref, sself_ref = rest
    else:
        rows_ref, q_ref, k_ref, v_ref = rest
    cs = cs_ref[...]
    sn = sn_ref[...]
    h = _rms(x_ref[0], gmix_ref[...]) * (1.0 + scale_ref[0]) + shift_ref[0]
    z = _bdot(h, w_in_ref[...])
    cq = _rms(z[:, :Q_LORA_RANK], g_q_lat_ref[...])
    lat = _rms(z[:, Q_LORA_RANK:Q_LORA_RANK + KV_LORA_RANK], g_kv_lat_ref[...])
    kr = _rope_pad(z[:, Q_LORA_RANK + KV_LORA_RANK:], g_kr_ref[...], cs, sn)
    rows_ref[0, :, :KV_LORA_RANK] = lat
    rows_ref[0, :, KV_LORA_RANK:] = (kr + pltpu.roll(kr, ROPE_PAD - ROPE_HALF, axis=1))[:, :QK_ROPE_DIM]
    v_ref[0] = lat.astype(BF16)
    q = _bdot(cq, w_uq_ref[...])
    kn_all = _bdot(lat, w_uk_ref[...])
    kr_b = kr.astype(BF16)
    if absorb:
        lane = lax.broadcasted_iota(jnp.int32, (x_ref.shape[1], 128), 1)
        sself = jnp.zeros((x_ref.shape[1], 128), F32)
    for hd in range(N_HEADS):
        o = hd * HEAD_PAD
        qn = _rms(q[:, o:o + QK_NOPE_DIM], g_qn_ref[...]) * ATTN_SCALE
        qr = _rope_pad(q[:, o + QK_NOPE_DIM:o + HEAD_PAD], g_qr_ref[...], cs, sn) * ATTN_SCALE
        kn = _rms(kn_all[:, hd * QK_NOPE_DIM:(hd + 1) * QK_NOPE_DIM], g_kn_ref[...])
        qn_b, qr_b, kn_b = qn.astype(BF16), qr.astype(BF16), kn.astype(BF16)
        q_ref[0, :, o:o + QK_NOPE_DIM] = qn_b
        q_ref[0, :, o + QK_NOPE_DIM:o + HEAD_PAD] = qr_b
        k_ref[0, :, o:o + QK_NOPE_DIM] = kn_b
        k_ref[0, :, o + QK_NOPE_DIM:o + HEAD_PAD] = kr_b
        if absorb:
            s = (jnp.sum(qn_b.astype(F32) * kn_b.astype(F32), axis=-1, keepdims=True)
                 + jnp.sum(qr_b.astype(F32) * kr_b.astype(F32), axis=-1, keepdims=True))
            sself = jnp.where(lane == hd, s, sself)
            qa = _bdot(qn * g_kn_ref[...], w_ukt_ref[hd])
            qa_ref[0, :, hd * KV_LORA_RANK:(hd + 1) * KV_LORA_RANK] = qa.astype(BF16)
    if absorb:
        sself_ref[0] = sself


def _mla_project(x, shift, scale, gmix, cs, sn, mw, tile, absorb):
    nb, s, _ = x.shape
    r = shift.shape[1]
    ns = s // tile
    rt = tile if r == s else 1
    tok = lambda bs: pl.BlockSpec(bs, lambda b, i: (b, i, 0))
    mod = pl.BlockSpec((1, rt, D_MODEL), (lambda b, i: (b, i, 0)) if r == s else (lambda b, i: (b, 0, 0)))
    full = lambda a: pl.BlockSpec(a.shape, lambda b, i: (0,) * a.ndim)
    weights = [mw["w_in"], mw["g_q_lat"], mw["g_kv_lat"], mw["g_kr"], mw["w_uq"], mw["g_qn"],
               mw["g_qr"], mw["w_uk"], mw["g_kn"]]
    if absorb:
        weights.append(mw["w_ukt"])
    out_shape = [jax.ShapeDtypeStruct((nb, s, CACHE_ROW), F32),
                 jax.ShapeDtypeStruct((nb, s, N_HEADS * HEAD_PAD), BF16),
                 jax.ShapeDtypeStruct((nb, s, N_HEADS * HEAD_PAD), BF16),
                 jax.ShapeDtypeStruct((nb, s, KV_LORA_RANK), BF16)]
    out_specs = [tok((1, tile, CACHE_ROW)), tok((1, tile, N_HEADS * HEAD_PAD)),
                 tok((1, tile, N_HEADS * HEAD_PAD)), tok((1, tile, KV_LORA_RANK))]
    if absorb:
        out_shape += [jax.ShapeDtypeStruct((nb, s, N_HEADS * KV_LORA_RANK), BF16),
                      jax.ShapeDtypeStruct((nb, s, 128), F32)]
        out_specs += [tok((1, tile, N_HEADS * KV_LORA_RANK)), tok((1, tile, 128))]
    return pl.pallas_call(
        functools.partial(_mla_proj_kernel, absorb),
        grid=(nb, ns),
        in_specs=[tok((1, tile, D_MODEL)), mod, mod, full(gmix),
                  pl.BlockSpec((tile, ROPE_PAD), lambda b, i: (i, 0)),
                  pl.BlockSpec((tile, ROPE_PAD), lambda b, i: (i, 0))] + [full(w) for w in weights],
        out_specs=out_specs,
        out_shape=out_shape,
        compiler_params=_cparams(2),
        name="mla_project_sample" if absorb else "mla_project",
    )(x, shift, scale, gmix, cs, sn, *weights)


def _attn_out(ctx_of_head, x, gate, w_uv_ref, w_out_ref):
    o = jnp.concatenate([_bdot(ctx_of_head(hd), w_uv_ref[hd]) for hd in range(N_HEADS)], axis=-1)
    return x + gate * _bdot(o, w_out_ref[...])


def _flash_kernel(q_ref, k_ref, v_ref, x_ref, gate_ref, w_uv_ref, w_out_ref, o_ref,
                  m_s, l_s, acc_s):
    qi = pl.program_id(1)
    ki = pl.program_id(2)
    tq = q_ref.shape[1]

    @pl.when(ki == 0)
    def _():
        m_s[...] = jnp.full(m_s.shape, NEG_INF, F32)
        l_s[...] = jnp.zeros(l_s.shape, F32)
        acc_s[...] = jnp.zeros(acc_s.shape, F32)

    def step(masked):
        v = v_ref[0]
        q_sub = min(Q_SUB, tq)
        for qs in range(tq // q_sub):
            rows = slice(qs * q_sub, (qs + 1) * q_sub)
            if masked:
                row = lax.broadcasted_iota(jnp.int32, (q_sub, tq), 0) + qs * q_sub
                col = lax.broadcasted_iota(jnp.int32, (q_sub, tq), 1)
                keep = col <= row
            for hd in range(N_HEADS):
                o = hd * HEAD_PAD
                s = _bdot_nt(q_ref[0, rows, o:o + HEAD_PAD], k_ref[0, :, o:o + HEAD_PAD])
                if masked:
                    s = jnp.where(keep, s, NEG_INF)
                m_prev = m_s[hd, rows]
                m_new = jnp.maximum(m_prev, jnp.max(s, axis=-1, keepdims=True))
                alpha = jnp.exp(m_prev - m_new)
                p = jnp.exp(s - m_new)
                l_s[hd, rows] = alpha * l_s[hd, rows] + jnp.sum(p, axis=-1, keepdims=True)
                acc_s[hd, rows] = alpha * acc_s[hd, rows] + _bdot(p, v)
                m_s[hd, rows] = m_new

    @pl.when(ki < qi)
    def _():
        step(False)

    @pl.when(ki == qi)
    def _():
        step(True)
        o_ref[0] = _attn_out(lambda hd: acc_s[hd] / l_s[hd], x_ref[0], gate_ref[0],
                             w_uv_ref, w_out_ref)


def _prompt_attention(q, k, v, x, gate, w_uv, w_out, tile):
    nb, s, _ = x.shape
    nt = s // tile
    full = lambda a: pl.BlockSpec(a.shape, lambda b, i, j: (0,) * a.ndim)
    return pl.pallas_call(
        _flash_kernel,
        grid=(nb, nt, nt),
        in_specs=[
            pl.BlockSpec((1, tile, N_HEADS * HEAD_PAD), lambda b, i, j: (b, i, 0)),
            pl.BlockSpec((1, tile, N_HEADS * HEAD_PAD), lambda b, i, j: (b, jnp.minimum(i, j), 0)),
            pl.BlockSpec((1, tile, KV_LORA_RANK), lambda b, i, j: (b, jnp.minimum(i, j), 0)),
            pl.BlockSpec((1, tile, D_MODEL), lambda b, i, j: (b, i, 0)),
            pl.BlockSpec((1, 1, D_MODEL), lambda b, i, j: (b, 0, 0)),
            full(w_uv), full(w_out),
        ],
        out_specs=pl.BlockSpec((1, tile, D_MODEL), lambda b, i, j: (b, i, 0)),
        out_shape=jax.ShapeDtypeStruct(x.shape, F32),
        scratch_shapes=[pltpu.VMEM((N_HEADS, tile, 1), F32), pltpu.VMEM((N_HEADS, tile, 1), F32),
                        pltpu.VMEM((N_HEADS, tile, KV_LORA_RANK), F32)],
        compiler_params=_cparams(3),
        name="prompt_attention",
    )(q, k, v, x, gate, w_uv, w_out)


PAGES_PER_STEP = 32
PAGES_PER_DOT = 4
QA_ROWS = 16


def _sample_attn_kernel(pt_ref, qa_ref, qr_ref, sself_ref, lat_new_ref, w_ukt_ref, *rest):
    del pt_ref
    pages = rest[:PAGES_PER_STEP]
    ctx_ref, m_s, l_s, acc_s, lat_s = rest[PAGES_PER_STEP:]
    step = pl.program_id(1)

    @pl.when(step == 0)
    def _():
        m_s[...] = sself_ref[0]
        l_s[...] = jnp.ones(l_s.shape, F32)
        acc_s[...] = jnp.broadcast_to(lat_new_ref[0], acc_s.shape)

    n_key_rows = N_HEADS * QK_NOPE_DIM
    n_dot = PAGES_PER_DOT * PAGE_SIZE
    lhs = jnp.concatenate([w_ukt_ref[...], qa_ref[0]], axis=0)
    qr = qr_ref[0]
    head_row = lax.broadcasted_iota(jnp.int32, (N_HEADS, n_dot), 0)
    scores = []
    for c in range(PAGES_PER_STEP // PAGES_PER_DOT):
        rows_t = jnp.concatenate([pages[c * PAGES_PER_DOT + p][...] for p in range(PAGES_PER_DOT)], axis=1)
        lat_t = rows_t[:KV_LORA_RANK].astype(BF16)
        kr_t = rows_t[KV_LORA_RANK:].astype(BF16)
        lat_s[:, c * n_dot:(c + 1) * n_dot] = lat_t
        yq = jnp.dot(lhs, lat_t, preferred_element_type=F32)
        ssq = jnp.zeros(head_row.shape, F32)
        for hd in range(N_HEADS):
            y = yq[hd * QK_NOPE_DIM:(hd + 1) * QK_NOPE_DIM]
            ssq = jnp.where(head_row == hd, jnp.sum(y * y, axis=0, keepdims=True), ssq)
        rinv = lax.rsqrt(ssq * (1.0 / QK_NOPE_DIM) + EPS)
        scores.append(yq[n_key_rows:n_key_rows + N_HEADS] * rinv
                      + jnp.dot(qr, kr_t, preferred_element_type=F32))
    s = jnp.concatenate(scores, axis=1)
    m_prev = m_s[...]
    m_new = jnp.maximum(m_prev, jnp.max(s, axis=-1, keepdims=True))
    alpha = jnp.exp(m_prev - m_new)
    p = jnp.exp(s - m_new)
    l_s[...] = alpha * l_s[...] + jnp.sum(p, axis=-1, keepdims=True)
    acc_s[...] = alpha * acc_s[...] + _bdot_nt(p, lat_s[...])
    m_s[...] = m_new

    @pl.when(step == pl.num_programs(1) - 1)
    def _():
        ctx_ref[0] = acc_s[...] / l_s[...]


def _sample_attention(cache_t, layer, page_table, qa, qr, sself, lat_new, w_ukt):
    nb, n_pages = page_table.shape
    n_steps = n_pages // PAGES_PER_STEP
    per_b = lambda a: pl.BlockSpec((1,) + a.shape[1:], lambda b, s, pt: (b,) + (0,) * (a.ndim - 1))
    full = lambda a: pl.BlockSpec(a.shape, lambda b, s, pt: (0,) * a.ndim)

    def page_spec(j):
        return pl.BlockSpec((None, None, CACHE_ROW, PAGE_SIZE),
                            lambda b, s, pt: (layer, pt[b * n_pages + s * PAGES_PER_STEP + j], 0, 0))

    grid_spec = pltpu.PrefetchScalarGridSpec(
        num_scalar_prefetch=1,
        grid=(nb, n_steps),
        in_specs=[per_b(qa), per_b(qr), per_b(sself), per_b(lat_new), full(w_ukt)]
        + [page_spec(j) for j in range(PAGES_PER_STEP)],
        out_specs=pl.BlockSpec((1, N_HEADS, KV_LORA_RANK), lambda b, s, pt: (b, 0, 0)),
        scratch_shapes=[pltpu.VMEM((N_HEADS, 1), F32), pltpu.VMEM((N_HEADS, 1), F32),
                        pltpu.VMEM((N_HEADS, KV_LORA_RANK), F32),
                        pltpu.VMEM((KV_LORA_RANK, PAGES_PER_STEP * PAGE_SIZE), BF16)],
    )
    return pl.pallas_call(
        _sample_attn_kernel,
        grid_spec=grid_spec,
        out_shape=jax.ShapeDtypeStruct((nb, N_HEADS, KV_LORA_RANK), F32),
        compiler_params=_cparams(2),
        name="sample_attention",
    )(page_table.reshape(-1), qa, qr, sself, lat_new, w_ukt, *([cache_t] * PAGES_PER_STEP))


def _sample_out_kernel(ctx_ref, x_ref, gate_ref, w_uv_ref, w_out_ref, o_ref):
    o_ref[...] = _attn_out(lambda hd: ctx_ref[:, hd * KV_LORA_RANK:(hd + 1) * KV_LORA_RANK],
                           x_ref[...], gate_ref[...], w_uv_ref, w_out_ref)


def _sample_out(ctx, x, gate, w_uv, w_out):
    return pl.pallas_call(
        _sample_out_kernel,
        out_shape=jax.ShapeDtypeStruct(x.shape, F32),
        compiler_params=pltpu.CompilerParams(vmem_limit_bytes=VMEM_LIMIT),
        name="sample_attention_out",
    )(ctx, x, gate, w_uv, w_out)


def _pool_mix(window_sum, h, cnt, w_pool_ref, pscale):
    parts = []
    for gi, w in enumerate(POOL_WINDOWS):
        ch = slice(gi * POOL_GROUP_DIM, (gi + 1) * POOL_GROUP_DIM)
        acc = h[:, ch]
        for kk in range(1, w):
            acc = acc + window_sum(kk, ch)
        d = acc / cnt[gi] - h[:, ch]
        parts.append(_bdot(d, w_pool_ref[gi]))
    return jnp.concatenate(parts, axis=-1) * pscale


def _pool_prompt_kernel(x_ref, shift_ref, scale_ref, gate_ref, gmix_ref, w_pool_ref, pscale_ref,
                        o_ref, st_ref, ext_s):
    si = pl.program_id(1)
    t = x_ref.shape[1]

    @pl.when(si == 0)
    def _():
        ext_s[0:POOL_HALO] = jnp.zeros((POOL_HALO, D_MODEL), F32)

    @pl.when(si > 0)
    def _():
        ext_s[0:POOL_HALO] = ext_s[t:t + POOL_HALO]

    x = x_ref[0]
    h = _rms(x, gmix_ref[...]) * (1.0 + scale_ref[0]) + shift_ref[0]
    ext_s[POOL_HALO:POOL_HALO + t] = h
    pos = si * t + lax.broadcasted_iota(jnp.int32, (t, 1), 0)
    cnt = [jnp.minimum(pos + 1, w).astype(F32) for w in POOL_WINDOWS]
    y = _pool_mix(lambda kk, ch: ext_s[POOL_HALO - kk:POOL_HALO - kk + t, ch], h, cnt,
                  w_pool_ref, pscale_ref[...])
    o_ref[0] = x + gate_ref[0] * y

    @pl.when(si == pl.num_programs(1) - 1)
    def _():
        st_ref[0] = ext_s[POOL_HALO + t - POOL_STATE_LEN:POOL_HALO + t]


def _pool_prompt(x, shift, scale, gate, gmix, w_pool, pscale, tile):
    nb, s, _ = x.shape
    tok = pl.BlockSpec((1, tile, D_MODEL), lambda b, i: (b, i, 0))
    mod = pl.BlockSpec((1, 1, D_MODEL), lambda b, i: (b, 0, 0))
    full = lambda a: pl.BlockSpec(a.shape, lambda b, i: (0,) * a.ndim)
    return pl.pallas_call(
        _pool_prompt_kernel,
        grid=(nb, s // tile),
        in_specs=[tok, mod, mod, mod, full(gmix), full(w_pool), full(pscale)],
        out_specs=[tok, pl.BlockSpec((1, POOL_STATE_LEN, D_MODEL), lambda b, i: (b, 0, 0))],
        out_shape=[jax.ShapeDtypeStruct(x.shape, F32),
                   jax.ShapeDtypeStruct((nb, POOL_STATE_LEN, D_MODEL), F32)],
        scratch_shapes=[pltpu.VMEM((POOL_HALO + tile, D_MODEL), F32)],
        compiler_params=_cparams(2),
        name="pool_prompt",
    )(x, shift, scale, gate, gmix, w_pool, pscale)


def _pool_sample_kernel(x_ref, shift_ref, scale_ref, gate_ref, gmix_ref, w_pool_ref, pscale_ref,
                        state_ref, o_ref, st_ref):
    x = x_ref[...]
    h = _rms(x, gmix_ref[...]) * (1.0 + scale_ref[...]) + shift_ref[...]
    cnt = [float(w) for w in POOL_WINDOWS]
    y = _pool_mix(lambda kk, ch: state_ref[POOL_STATE_LEN - kk, :, ch], h, cnt,
                  w_pool_ref, pscale_ref[...])
    o_ref[...] = x + gate_ref[...] * y
    for r in range(POOL_STATE_LEN - 1):
        st_ref[r] = state_ref[r + 1]
    st_ref[POOL_STATE_LEN - 1] = h


def _pool_sample(x, shift, scale, gate, gmix, w_pool, pscale, state_t):
    return pl.pallas_call(
        _pool_sample_kernel,
        out_shape=[jax.ShapeDtypeStruct(x.shape, F32), jax.ShapeDtypeStruct(state_t.shape, F32)],
        compiler_params=pltpu.CompilerParams(vmem_limit_bytes=VMEM_LIMIT),
        name="pool_sample",
    )(x, shift, scale, gate, gmix, w_pool, pscale, state_t)


NOT_TOP = 64.0
LANES = 128
SUBLANES = 8


def _sort_pairs(n):
    pairs, p = [], 1
    while p < n:
        k = p
        while k >= 1:
            for j in range(k % p, n - k, 2 * k):
                for i in range(min(k, n - j - k)):
                    if (i + j) // (2 * p) == (i + j + k) // (2 * p):
                        pairs.append((i + j, i + j + k))
            k //= 2
        p *= 2
    return pairs


def _bitonic_merge_pairs(n):
    pairs, k = [], n // 2
    while k >= 1:
        pairs += [(i, i + k) for i in range(n) if not i & k]
        k //= 2
    return pairs


def _compare_exchange(v, pairs):
    for i, j in pairs:
        v[i], v[j] = jnp.maximum(v[i], v[j]), jnp.minimum(v[i], v[j])
    return v


def _top16_sorted(s):
    v = [s[r * SUBLANES:(r + 1) * SUBLANES] for r in range(PEER_TOPK)]
    v = _compare_exchange(v, _sort_pairs(PEER_TOPK))
    for shift in (4, 2, 1):
        v = [jnp.maximum(v[i], pltpu.roll(v[PEER_TOPK - 1 - i], shift, axis=0)) for i in range(PEER_TOPK)]
        v = _compare_exchange(v, _bitonic_merge_pairs(PEER_TOPK))
    return v


def _kth_largest(work, k):
    for _ in range(k):
        m = jnp.max(work, axis=0, keepdims=True)
        work = jnp.where(work == m, NEG_INF, work)
    return m


def _prefix_count(x, thresholds, above):
    parts = []
    for r in range(x.shape[0] // SUBLANES):
        xr = x[r * SUBLANES:(r + 1) * SUBLANES]
        cnt = jnp.zeros(xr.shape, F32)
        for b, th in enumerate(thresholds):
            cnt = jnp.where(th > xr if above else xr >= th, float(b + 1), cnt)
        parts.append(cnt)
    return jnp.concatenate(parts, axis=0)


def _dup_bf16(x):
    hi = pltpu.bitcast(x.astype(BF16).astype(F32), jnp.uint32)
    return hi | (hi >> 16)


def _pack_pairs(x):
    return pltpu.bitcast(x.astype(BF16), jnp.uint32)


def _peer_route_kernel(x_ref, shift_ref, scale_ref, gffn_ref, wqt_ref, subk_ref,
                       ht_ref, r2_ref, e2_ref, n1_ref, e1_ref):
    t = x_ref.shape[1]
    h = _rms(x_ref[0], gffn_ref[...]) * (1.0 + scale_ref[0]) + shift_ref[0]
    ht = h.T.astype(BF16)
    ht_ref[...] = pltpu.bitcast(ht, jnp.uint32)
    qt = jnp.dot(wqt_ref[...], ht, preferred_element_type=F32).astype(BF16)
    row8 = lax.broadcasted_iota(jnp.int32, (SUBLANES, LANES), 0)
    for hd in range(PEER_HEADS):
        sc = [jnp.dot(subk_ref[2 * hd + p], qt[(2 * hd + p) * PEER_HALF:(2 * hd + p + 1) * PEER_HALF],
                      preferred_element_type=F32) for p in range(2)]
        for lt in range(t // LANES):
            ln = slice(lt * LANES, (lt + 1) * LANES)
            s1, s2 = sc[0][:, ln], sc[1][:, ln]
            top1, top2 = _top16_sorted(s1), _top16_sorted(s2)
            v1_lo, v1_hi = top1[0], top1[SUBLANES]
            for a in range(1, SUBLANES):
                v1_lo = jnp.where(row8 == a, top1[a], v1_lo)
                v1_hi = jnp.where(row8 == a, top1[SUBLANES + a], v1_hi)
            t1, t2 = top1, top2
            below = lambda r: row8 < r
            cand = jnp.concatenate([
                v1_lo + t2[0],
                v1_hi + t2[0],
                v1_lo + t2[1],
                jnp.where(below(5), v1_lo, jnp.where(row8 == 5, t1[0], jnp.where(row8 == 6, t1[1], t1[2])))
                + jnp.where(below(5), t2[2], t2[4]),
                jnp.where(below(4), v1_lo, jnp.where((row8 & 1) == 0, t1[0], t1[1]))
                + jnp.where(below(4), t2[3], jnp.where(below(6), t2[5], t2[6])),
                jnp.where(below(2), v1_lo, t1[0])
                + jnp.where(below(2), t2[7], jnp.where(below(3), t2[8], jnp.where(below(4), t2[9], jnp.where(
                    below(5), t2[10], jnp.where(below(6), t2[11], jnp.where(below(7), t2[12], t2[13])))))),
                t1[0] + jnp.where(below(1), t2[14], jnp.where(below(2), t2[15], NEG_INF)),
            ], axis=0)
            tau = _kth_largest(cand, PEER_TOPK)
            cmax = top1[0] + top2[0]
            z = jnp.sum(jnp.where(cand >= tau, jnp.exp(cand - cmax[:1]), 0.0), axis=0, keepdims=True)
            theta = []
            for b in range(PEER_TOPK):
                lo = jnp.where(v1_lo + top2[b] >= tau, v1_lo, -NEG_INF)
                hi = jnp.where(v1_hi + top2[b] >= tau, v1_hi, -NEG_INF)
                theta.append(jnp.min(jnp.minimum(lo, hi), axis=0, keepdims=True))
            n1 = _prefix_count(s1, theta, above=False)
            rank2 = _prefix_count(s2, top2, above=True)
            rank2 = jnp.where(rank2 >= PEER_TOPK, NOT_TOP, rank2)
            r2_ref[hd, :, ln] = _pack_pairs(rank2)
            e2_ref[hd, :, ln] = _pack_pairs(jnp.exp(s2 - top2[0][:1]))
            n1_ref[hd, lt] = _dup_bf16(n1)
            e1_ref[hd, lt] = _dup_bf16(jnp.exp(s1 - top1[0][:1]) * (0.5 / z))


def _peer_route(x, shift, scale, gffn, wqt, subk, tile):
    nb, s, _ = x.shape
    r = shift.shape[1]
    ns = s // tile
    n = nb * s
    rt = tile if r == s else 1
    tok = pl.BlockSpec((1, tile, D_MODEL), lambda b, i: (b, i, 0))
    mod = pl.BlockSpec((1, rt, D_MODEL), (lambda b, i: (b, i, 0)) if r == s else (lambda b, i: (b, 0, 0)))
    full = lambda a: pl.BlockSpec(a.shape, lambda b, i: (0,) * a.ndim)
    u32 = jnp.uint32
    lane_tiles = tile // LANES
    pairs = pl.BlockSpec((PEER_HEADS, PEER_N_KEYS // 2, tile), lambda b, i: (0, 0, b * ns + i))
    rows = pl.BlockSpec((PEER_HEADS, lane_tiles, PEER_N_KEYS, LANES), lambda b, i: (0, b * ns + i, 0, 0))
    pairs_shape = jax.ShapeDtypeStruct((PEER_HEADS, PEER_N_KEYS // 2, n), u32)
    rows_shape = jax.ShapeDtypeStruct((PEER_HEADS, n // LANES, PEER_N_KEYS, LANES), u32)
    return pl.pallas_call(
        _peer_route_kernel,
        grid=(nb, ns),
        in_specs=[tok, mod, mod, full(gffn), full(wqt), full(subk)],
        out_specs=[pl.BlockSpec((D_MODEL // 2, tile), lambda b, i: (0, b * ns + i)),
                   pairs, pairs, rows, rows],
        out_shape=[jax.ShapeDtypeStruct((D_MODEL // 2, n), u32), pairs_shape, pairs_shape,
                   rows_shape, rows_shape],
        compiler_params=_cparams(2),
        name="peer_route",
    )(x, shift, scale, gffn, wqt, subk)


N_CHUNKS = PEER_N_EXPERTS // EXPERT_CHUNK
GATE_ROWS = 32
OUT_DEPTH = 1024
OUT_ROWS = 1024
KEYS_PER_PIECE = 8


def _peer_sweep_kernel(ht_ref, r2_ref, e2_ref, n1_ref, e1_ref, u_ref, vt_ref, x_ref, gate_ref, o_ref,
                       ga_s, yt_s):
    j = pl.program_id(2)
    t = ht_ref.shape[1]

    @pl.when(j == 0)
    def _():
        yt_s[...] = jnp.zeros(yt_s.shape, F32)

    ht = pltpu.bitcast(ht_ref[...], BF16)
    zero = jnp.zeros((GATE_ROWS, LANES), BF16)
    n_groups = PEER_N_KEYS // GATE_ROWS

    def gate_piece(k):
        n_rows = KEYS_PER_PIECE * PEER_N_KEYS
        u = pltpu.bitcast(u_ref[k * n_rows // 2:(k + 1) * n_rows // 2, :], BF16)
        act = jnp.dot(u, ht, preferred_element_type=F32)
        ga_s[k * n_rows:(k + 1) * n_rows, :] = (act * (1.0 + lax.erf(act * (2.0 ** -0.5)))).astype(BF16)
        for lt in range(t // LANES):
            ln = slice(lt * LANES, (lt + 1) * LANES)
            g = [[zero] * n_groups for _ in range(KEYS_PER_PIECE)]
            for hd in range(PEER_HEADS):
                n1, e1 = [], []
                for kk in range(KEYS_PER_PIECE):
                    row = pl.ds(j * KEY_ROWS_PER_CHUNK + k * KEYS_PER_PIECE + kk, 1)
                    spread = lambda ref: pltpu.bitcast(
                        jnp.broadcast_to(ref[hd, lt, row, :], (GATE_ROWS // 2, LANES)), BF16)
                    n1.append(spread(n1_ref))
                    e1.append(spread(e1_ref))
                for sg in range(n_groups):
                    pairs = slice(sg * GATE_ROWS // 2, (sg + 1) * GATE_ROWS // 2)
                    r2 = pltpu.bitcast(r2_ref[hd, pairs, ln], BF16)
                    e2 = pltpu.bitcast(e2_ref[hd, pairs, ln], BF16)
                    for kk in range(KEYS_PER_PIECE):
                        g[kk][sg] = g[kk][sg] + jnp.where(r2 < n1[kk], e2 * e1[kk], zero)
            for kk in range(KEYS_PER_PIECE):
                for sg in range(n_groups):
                    r0 = k * n_rows + kk * PEER_N_KEYS + sg * GATE_ROWS
                    ga_s[r0:r0 + GATE_ROWS, ln] = ga_s[r0:r0 + GATE_ROWS, ln] * g[kk][sg]

    def out_piece(group, m):
        cols = slice(group * OUT_DEPTH, (group + 1) * OUT_DEPTH)
        rows = slice(m * OUT_ROWS, (m + 1) * OUT_ROWS)
        v = pltpu.bitcast(vt_ref[m * OUT_ROWS // 2:(m + 1) * OUT_ROWS // 2, cols], BF16)
        yt_s[rows, :] += jnp.dot(v, ga_s[cols, :], preferred_element_type=F32)

    pieces_per_group = OUT_DEPTH // (KEYS_PER_PIECE * PEER_N_KEYS)
    n_pieces = KEY_ROWS_PER_CHUNK // KEYS_PER_PIECE
    pending = []
    for k in range(n_pieces):
        gate_piece(k)
        if pending:
            out_piece(*pending.pop(0))
        if (k + 1) % pieces_per_group == 0:
            pending += [(k // pieces_per_group, m) for m in range(D_MODEL // OUT_ROWS)]
    for piece in pending:
        out_piece(*piece)

    @pl.when(j == pl.num_programs(2) - 1)
    def _():
        o_ref[0] = x_ref[0] + gate_ref[0] * yt_s[...].T


def _peer_sweep(ht, r2, e2, n1, e1, u, vt, x, gate, tile):
    nb, s, _ = x.shape
    r = gate.shape[1]
    ns = s // tile
    pairs = pl.BlockSpec((PEER_HEADS, PEER_N_KEYS // 2, tile), lambda b, i, j: (0, 0, b * ns + i))
    rows = pl.BlockSpec((PEER_HEADS, tile // LANES, PEER_N_KEYS, LANES),
                        lambda b, i, j: (0, b * ns + i, 0, 0))
    tok = pl.BlockSpec((1, tile, D_MODEL), lambda b, i, j: (b, i, 0))
    mod = pl.BlockSpec((1, tile if r == s else 1, D_MODEL),
                       (lambda b, i, j: (b, i, 0)) if r == s else (lambda b, i, j: (b, 0, 0)))
    return pl.pallas_call(
        _peer_sweep_kernel,
        grid=(nb, ns, N_CHUNKS),
        in_specs=[pl.BlockSpec((D_MODEL // 2, tile), lambda b, i, j: (0, b * ns + i)),
                  pairs, pairs, rows, rows,
                  pl.BlockSpec((EXPERT_CHUNK // 2, D_MODEL), lambda b, i, j: (j, 0)),
                  pl.BlockSpec((D_MODEL // 2, EXPERT_CHUNK), lambda b, i, j: (0, j)),
                  tok, mod],
        out_specs=tok,
        out_shape=jax.ShapeDtypeStruct(x.shape, F32),
        scratch_shapes=[pltpu.VMEM((EXPERT_CHUNK, tile), BF16), pltpu.VMEM((D_MODEL, tile), F32)],
        compiler_params=_cparams(3),
        name="peer_sweep",
    )(ht, r2, e2, n1, e1, u, vt, x, gate)


def _peer(x, shift, scale, gate, gffn, pw, route_tile, sweep_tile):
    routed = _peer_route(x, shift, scale, gffn, pw["wqt"], pw["subk"], route_tile)
    return _peer_sweep(*routed, pw["u"], pw["vt"], x, gate, sweep_tile)


def _rope_tables(pos):
    inv = ROPE_THETA ** (-jnp.arange(0, QK_ROPE_DIM, 2, dtype=F32) / QK_ROPE_DIM)
    ang = pos.astype(F32)[:, None] * inv[None, :]
    c, s, z = jnp.cos(ang), jnp.sin(ang), jnp.zeros_like(ang)
    return jnp.concatenate([c, z, c, z], -1), jnp.concatenate([-s, z, s, z], -1)


def _pad_rope_cols(w):
    z = jnp.zeros(w.shape[:-1] + (ROPE_HALF,), w.dtype)
    return jnp.concatenate([w[..., :ROPE_HALF], z, w[..., ROPE_HALF:], z], -1)


def _mla_weights(w_in, g_q_lat, g_kv_lat, w_uq, w_uk, w_uv, g_qn_nope, g_qn_rope, g_kn_nope,
                 g_kn_rope, w_out):
    lat_end = Q_LORA_RANK + KV_LORA_RANK
    w_in_p = jnp.concatenate([w_in[:, :lat_end], _pad_rope_cols(w_in[:, lat_end:])], -1)
    wq = w_uq.reshape(Q_LORA_RANK, N_HEADS, QK_NOPE_DIM + QK_ROPE_DIM)
    wq = jnp.concatenate([wq[..., :QK_NOPE_DIM], _pad_rope_cols(wq[..., QK_NOPE_DIM:])], -1)
    row = lambda g: g.reshape(1, -1)
    return {
        "w_in": w_in_p.astype(BF16),
        "g_q_lat": row(g_q_lat), "g_kv_lat": row(g_kv_lat), "g_kr": row(_pad_rope_cols(g_kn_rope)),
        "w_uq": wq.reshape(Q_LORA_RANK, N_HEADS * HEAD_PAD).astype(BF16),
        "g_qn": row(g_qn_nope), "g_qr": row(_pad_rope_cols(g_qn_rope)),
        "w_uk": w_uk.reshape(KV_LORA_RANK, N_HEADS * QK_NOPE_DIM).astype(BF16),
        "g_kn": row(g_kn_nope),
        "w_ukt": jnp.transpose(w_uk, (1, 2, 0)).astype(BF16),
        "w_uv": jnp.transpose(w_uv, (1, 0, 2)).astype(BF16),
        "w_out": w_out.astype(BF16),
    }


def _pack_rows_kernel(x_ref, o_ref):
    o_ref[...] = _pack_pairs(x_ref[...])


def _pack_cols_kernel(x_ref, o_ref):
    o_ref[...] = _pack_pairs(x_ref[...].T)


def _pack_weight(w, transpose):
    r, c = w.shape
    blk = 1024
    if transpose:
        return pl.pallas_call(
            _pack_cols_kernel, grid=(r // blk,),
            in_specs=[pl.BlockSpec((blk, c), lambda i: (i, 0))],
            out_specs=pl.BlockSpec((c // 2, blk), lambda i: (0, i)),
            out_shape=jax.ShapeDtypeStruct((c // 2, r), jnp.uint32),
            compiler_params=_cparams(1), name="pack_weight_t")(w)
    return pl.pallas_call(
        _pack_rows_kernel, grid=(r // blk,),
        in_specs=[pl.BlockSpec((blk, c), lambda i: (i, 0))],
        out_specs=pl.BlockSpec((blk // 2, c), lambda i: (i, 0)),
        out_shape=jax.ShapeDtypeStruct((r // 2, c), jnp.uint32),
        compiler_params=_cparams(1), name="pack_weight")(w)


def _peer_weights(w_q, subkeys, u, v):
    return {
        "wqt": w_q.T.astype(BF16),
        "subk": subkeys.reshape(2 * PEER_HEADS, PEER_N_KEYS, PEER_HALF).astype(BF16),
        "u": _pack_weight(u, False),
        "vt": _pack_weight(v, True),
    }


PROMPT_TILE = 256
SAMPLE_TILE = 128
PEER_SWEEP_TILE = 512
ATTN_TILE = 512
Q_SUB = 256


def kernel(x_prompt, x_sample, c_prompt, c_sample, cache_mla, state_pool, page_table, norm_mix, norm_ffn, w_ada, b_ada, w_mla_in, g_q_lat, g_kv_lat, w_uq, w_uk, w_uv, g_qn_nope, g_qn_rope, g_kn_nope, g_kn_rope, w_mla_out, w_pool, pool_scale, w_peer_q, peer_subkeys, peer_u, peer_v):
    nbp, seq, _ = x_prompt.shape
    nbs = x_sample.shape[0]
    depth = w_ada.shape[0]
    xp = x_prompt
    xs = x_sample.reshape(1, nbs, D_MODEL)

    mods = _ada_params(jnp.concatenate([c_prompt, c_sample], 0), w_ada, b_ada)
    mods = mods.reshape(depth, nbp + nbs, 6, D_MODEL)
    cs_p, sn_p = _rope_tables(jnp.arange(seq))
    cs_s, sn_s = _rope_tables(jnp.full((nbs,), PAST_LEN))
    cache_t = jnp.swapaxes(cache_mla, 2, 3)

    rows_p, rows_s, pst_p, pst_s = [], [], [], []
    for l in range(depth):
        mp = [mods[l, :nbp, k].reshape(nbp, 1, D_MODEL) for k in range(6)]
        ms = [mods[l, nbp:, k].reshape(1, nbs, D_MODEL) for k in range(6)]
        gmix = norm_mix[l].reshape(1, D_MODEL)
        gffn = norm_ffn[l].reshape(1, D_MODEL)
        i = l // 2
        if l % 2 == 0:
            mw = _mla_weights(w_mla_in[i], g_q_lat[i], g_kv_lat[i], w_uq[i], w_uk[i], w_uv[i],
                              g_qn_nope[i], g_qn_rope[i], g_kn_nope[i], g_kn_rope[i], w_mla_out[i])
            r_p, q, k, v = _mla_project(xp, mp[0], mp[1], gmix, cs_p, sn_p, mw, PROMPT_TILE, False)
            xp = _prompt_attention(q, k, v, xp, mp[2], mw["w_uv"], mw["w_out"], ATTN_TILE)
            r_s, q, k, v, qa, sself = _mla_project(xs, ms[0], ms[1], gmix, cs_s, sn_s, mw,
                                                   SAMPLE_TILE, True)
            q = q.reshape(nbs, N_HEADS, HEAD_PAD)[:, :, QK_NOPE_DIM:]
            qr = jnp.concatenate([q[..., :ROPE_HALF], q[..., 2 * ROPE_HALF:3 * ROPE_HALF]], -1)
            qa = jnp.pad(qa.reshape(nbs, N_HEADS, KV_LORA_RANK), ((0, 0), (0, QA_ROWS - N_HEADS), (0, 0)))
            ctx = _sample_attention(
                cache_t, i, page_table, qa, qr, sself.reshape(nbs, 128)[:, :N_HEADS, None],
                r_s.reshape(nbs, 1, CACHE_ROW)[..., :KV_LORA_RANK],
                mw["w_ukt"].reshape(N_HEADS * QK_NOPE_DIM, KV_LORA_RANK))
            xs = _sample_out(ctx.reshape(nbs, N_HEADS * KV_LORA_RANK), xs[0], ms[2][0],
                             mw["w_uv"], mw["w_out"]).reshape(1, nbs, D_MODEL)
            rows_p.append(r_p)
            rows_s.append(r_s.reshape(nbs, 1, CACHE_ROW))
        else:
            wp = w_pool[i].astype(BF16)
            psc = pool_scale[i].reshape(1, D_MODEL)
            xp, st_p = _pool_prompt(xp, mp[0], mp[1], mp[2], gmix, wp, psc, PROMPT_TILE)
            x2, st_s = _pool_sample(xs[0], ms[0][0], ms[1][0], ms[2][0], gmix, wp, psc,
                                    jnp.transpose(state_pool[i], (1, 0, 2)))
            xs = x2.reshape(1, nbs, D_MODEL)
            pst_p.append(st_p)
            pst_s.append(jnp.transpose(st_s, (1, 0, 2)))
        pw = _peer_weights(w_peer_q[l], peer_subkeys[l], peer_u[l], peer_v[l])
        xp = _peer(xp, mp[3], mp[4], mp[5], gffn, pw, PROMPT_TILE, PEER_SWEEP_TILE)
        xs = _peer(xs, ms[3], ms[4], ms[5], gffn, pw, SAMPLE_TILE, SAMPLE_TILE)
    return (xp, xs.reshape(nbs, 1, D_MODEL), jnp.stack(rows_p), jnp.stack(rows_s),
            jnp.stack(pst_p), jnp.stack(pst_s))
```

```python
import functools

import jax
import jax.numpy as jnp
from jax import lax
from jax.experimental import pallas as pl
from jax.experimental.pallas import tpu as pltpu

F32 = jnp.float32
BF16 = jnp.bfloat16

D_MODEL = 1024
EPS = 1e-6
PAST_LEN = 8192
PAGE_SIZE = 128
N_HEADS = 8
QK_NOPE_DIM = 128
QK_ROPE_DIM = 64
ROPE_HALF = QK_ROPE_DIM // 2
ROPE_PAD = 128
HEAD_PAD = QK_NOPE_DIM + ROPE_PAD
V_HEAD_DIM = 128
Q_LORA_RANK = 384
KV_LORA_RANK = 256
CACHE_ROW = KV_LORA_RANK + QK_ROPE_DIM
ROPE_THETA = 10000.0
ATTN_SCALE = (QK_NOPE_DIM + QK_ROPE_DIM) ** -0.5
LOG2_E = 1.4426950408889634
POOL_WINDOWS = (2, 4, 8, 16)
POOL_GROUP_DIM = D_MODEL // len(POOL_WINDOWS)
POOL_STATE_LEN = max(POOL_WINDOWS) - 1
POOL_HALO = 16
PEER_HEADS = 8
PEER_N_KEYS = 128
PEER_N_EXPERTS = PEER_N_KEYS * PEER_N_KEYS
PEER_TOPK = 16
PEER_HALF = 128
EXPERT_CHUNK = 2048
KEY_ROWS_PER_CHUNK = EXPERT_CHUNK // PEER_N_KEYS

VMEM_LIMIT = 56 * 1024 * 1024

NEG_INF = float("-inf")


def _cparams(n_grid):
    return pltpu.CompilerParams(
        dimension_semantics=("arbitrary",) * n_grid, vmem_limit_bytes=VMEM_LIMIT)


def _rms(x, g, n=None):
    n = x.shape[-1] if n is None else n
    ms = jnp.sum(x * x, axis=-1, keepdims=True) * (1.0 / n)
    return x * lax.rsqrt(ms + EPS) * g


def _bdot(a, b):
    return jnp.dot(a.astype(BF16), b.astype(BF16), preferred_element_type=F32)


def _bdot_nt(a, b):
    return lax.dot_general(a.astype(BF16), b.astype(BF16), (((1,), (1,)), ((), ())),
                           preferred_element_type=F32)


def _mods_kernel(c_ref, w_ref, b_ref, o_ref):
    o_ref[0] = _bdot(jax.nn.silu(c_ref[...]), w_ref[0]) + b_ref[0]


def _ada_params(c_all, w_ada, b_ada):
    n_layers = w_ada.shape[0]
    n_rows = c_all.shape[0]
    n_col = w_ada.shape[2] // D_MODEL
    return pl.pallas_call(
        _mods_kernel,
        grid=(n_layers, n_col),
        in_specs=[
            pl.BlockSpec((n_rows, D_MODEL), lambda l, k: (0, 0)),
            pl.BlockSpec((1, D_MODEL, D_MODEL), lambda l, k: (l, 0, k)),
            pl.BlockSpec((1, 1, D_MODEL), lambda l, k: (l, 0, k)),
        ],
        out_specs=pl.BlockSpec((1, n_rows, D_MODEL), lambda l, k: (l, 0, k)),
        out_shape=jax.ShapeDtypeStruct((n_layers, n_rows, n_col * D_MODEL), F32),
        compiler_params=_cparams(2),
        name="ada_params",
    )(c_all, w_ada, b_ada.reshape(n_layers, 1, -1))


def _rope_pad(x, g, cs, sn):
    y = _rms(x, g, QK_ROPE_DIM)
    return y * cs + pltpu.roll(y, ROPE_PAD // 2, axis=1) * sn


def _mla_proj_kernel(absorb, x_ref, shift_ref, scale_ref, gmix_ref, cs_ref, sn_ref, w_in_ref,
                     g_q_lat_ref, g_kv_lat_ref, g_kr_ref, w_uq_ref, g_qn_ref, g_qr_ref,
                     w_uk_ref, g_kn_ref, *rest):
    if absorb:
        w_ukt_ref, rows_ref, q_ref, k_ref, v_ref, qa_ref, sself_ref = rest
    else:
        rows_ref, q_ref, k_ref, v_ref = rest
    cs = cs_ref[...]
    sn = sn_ref[...]
    h = _rms(x_ref[0], gmix_ref[...]) * (1.0 + scale_ref[0]) + shift_ref[0]
    z = _bdot(h, w_in_ref[...])
    cq = _rms(z[:, :Q_LORA_RANK], g_q_lat_ref[...])
    lat = _rms(z[:, Q_LORA_RANK:Q_LORA_RANK + KV_LORA_RANK], g_kv_lat_ref[...])
    kr = _rope_pad(z[:, Q_LORA_RANK + KV_LORA_RANK:], g_kr_ref[...], cs, sn)
    rows_ref[0, :, :KV_LORA_RANK] = lat
    rows_ref[0, :, KV_LORA_RANK:] = (kr + pltpu.roll(kr, ROPE_PAD - ROPE_HALF, axis=1))[:, :QK_ROPE_DIM]
    v_ref[0] = lat.astype(BF16)
    q = _bdot(cq, w_uq_ref[...])
    kn_all = _bdot(lat, w_uk_ref[...])
    kr_b = kr.astype(BF16)
    if absorb:
        lane = lax.broadcasted_iota(jnp.int32, (x_ref.shape[1], 128), 1)
        sself = jnp.zeros((x_ref.shape[1], 128), F32)
    for hd in range(N_HEADS):
        o = hd * HEAD_PAD
        qn = _rms(q[:, o:o + QK_NOPE_DIM], g_qn_ref[...]) * (ATTN_SCALE * LOG2_E)
        qr = _rope_pad(q[:, o + QK_NOPE_DIM:o + HEAD_PAD], g_qr_ref[...], cs, sn) * (ATTN_SCALE * LOG2_E)
        kn = _rms(kn_all[:, hd * QK_NOPE_DIM:(hd + 1) * QK_NOPE_DIM], g_kn_ref[...])
        qn_b, qr_b, kn_b = qn.astype(BF16), qr.astype(BF16), kn.astype(BF16)
        q_ref[0, :, o:o + QK_NOPE_DIM] = qn_b
        q_ref[0, :, o + QK_NOPE_DIM:o + HEAD_PAD] = qr_b
        k_ref[0, :, o:o + QK_NOPE_DIM] = kn_b
        k_ref[0, :, o + QK_NOPE_DIM:o + HEAD_PAD] = kr_b
        if absorb:
            s = (jnp.sum(qn_b.astype(F32) * kn_b.astype(F32), axis=-1, keepdims=True)
                 + jnp.sum(qr_b.astype(F32) * kr_b.astype(F32), axis=-1, keepdims=True))
            sself = jnp.where(lane == hd, s, sself)
            qa = _bdot(qn * g_kn_ref[...], w_ukt_ref[hd])
            qa_ref[0, :, hd * KV_LORA_RANK:(hd + 1) * KV_LORA_RANK] = qa.astype(BF16)
    if absorb:
        sself_ref[0] = sself


def _mla_project(x, shift, scale, gmix, cs, sn, mw, tile, absorb):
    nb, s, _ = x.shape
    r = shift.shape[1]
    ns = s // tile
    rt = tile if r == s else 1
    tok = lambda bs: pl.BlockSpec(bs, lambda b, i: (b, i, 0))
    mod = pl.BlockSpec((1, rt, D_MODEL), (lambda b, i: (b, i, 0)) if r == s else (lambda b, i: (b, 0, 0)))
    full = lambda a: pl.BlockSpec(a.shape, lambda b, i: (0,) * a.ndim)
    weights = [mw["w_in"], mw["g_q_lat"], mw["g_kv_lat"], mw["g_kr"], mw["w_uq"], mw["g_qn"],
               mw["g_qr"], mw["w_uk"], mw["g_kn"]]
    if absorb:
        weights.append(mw["w_ukt"])
    out_shape = [jax.ShapeDtypeStruct((nb, s, CACHE_ROW), F32),
                 jax.ShapeDtypeStruct((nb, s, N_HEADS * HEAD_PAD), BF16),
                 jax.ShapeDtypeStruct((nb, s, N_HEADS * HEAD_PAD), BF16),
                 jax.ShapeDtypeStruct((nb, s, KV_LORA_RANK), BF16)]
    out_specs = [tok((1, tile, CACHE_ROW)), tok((1, tile, N_HEADS * HEAD_PAD)),
                 tok((1, tile, N_HEADS * HEAD_PAD)), tok((1, tile, KV_LORA_RANK))]
    if absorb:
        out_shape += [jax.ShapeDtypeStruct((nb, s, N_HEADS * KV_LORA_RANK), BF16),
                      jax.ShapeDtypeStruct((nb, s, 128), F32)]
        out_specs += [tok((1, tile, N_HEADS * KV_LORA_RANK)), tok((1, tile, 128))]
    return pl.pallas_call(
        functools.partial(_mla_proj_kernel, absorb),
        grid=(nb, ns),
        in_specs=[tok((1, tile, D_MODEL)), mod, mod, full(gmix),
                  pl.BlockSpec((tile, ROPE_PAD), lambda b, i: (i, 0)),
                  pl.BlockSpec((tile, ROPE_PAD), lambda b, i: (i, 0))] + [full(w) for w in weights],
        out_specs=out_specs,
        out_shape=out_shape,
        compiler_params=_cparams(2),
        name="mla_project_sample" if absorb else "mla_project",
    )(x, shift, scale, gmix, cs, sn, *weights)


def _attn_out(ctx_of_head, x, gate, w_uv_ref, w_out_ref):
    o = jnp.concatenate([_bdot(ctx_of_head(hd), w_uv_ref[hd]) for hd in range(N_HEADS)], axis=-1)
    return x + gate * _bdot(o, w_out_ref[...])


def _flash_kernel(q_ref, k_ref, v_ref, x_ref, gate_ref, w_uv_ref, w_out_ref, o_ref,
                  m_s, l_s, acc_s):
    qi = pl.program_id(1)
    ki = pl.program_id(2)
    tq = q_ref.shape[1]

    @pl.when(ki == 0)
    def _():
        m_s[...] = jnp.full(m_s.shape, NEG_INF, F32)
        l_s[...] = jnp.zeros(l_s.shape, F32)
        acc_s[...] = jnp.zeros(acc_s.shape, F32)

    def step(masked):
        v = v_ref[0]
        q_sub = min(Q_SUB, tq)
        for qs in range(tq // q_sub):
            rows = slice(qs * q_sub, (qs + 1) * q_sub)
            if masked:
                row = lax.broadcasted_iota(jnp.int32, (q_sub, tq), 0) + qs * q_sub
                col = lax.broadcasted_iota(jnp.int32, (q_sub, tq), 1)
                keep = col <= row
            for hd in range(N_HEADS):
                o = hd * HEAD_PAD
                s = _bdot_nt(q_ref[0, rows, o:o + HEAD_PAD], k_ref[0, :, o:o + HEAD_PAD])
                if masked:
                    s = jnp.where(keep, s, NEG_INF)
                m_prev = m_s[hd, rows]
                m_new = jnp.maximum(m_prev, jnp.max(s, axis=-1, keepdims=True))
                alpha = jnp.exp2(m_prev - m_new)
                p = jnp.exp2(s - m_new)
                l_s[hd, rows] = alpha * l_s[hd, rows] + jnp.sum(p, axis=-1, keepdims=True)
                acc_s[hd, rows] = alpha * acc_s[hd, rows] + _bdot(p, v)
                m_s[hd, rows] = m_new

    @pl.when(ki < qi)
    def _():
        step(False)

    @pl.when(ki == qi)
    def _():
        step(True)
        o_ref[0] = _attn_out(lambda hd: acc_s[hd] / l_s[hd], x_ref[0], gate_ref[0],
                             w_uv_ref, w_out_ref)


def _prompt_attention(q, k, v, x, gate, w_uv, w_out, tile):
    nb, s, _ = x.shape
    nt = s // tile
    full = lambda a: pl.BlockSpec(a.shape, lambda b, i, j: (0,) * a.ndim)
    return pl.pallas_call(
        _flash_kernel,
        grid=(nb, nt, nt),
        in_specs=[
            pl.BlockSpec((1, tile, N_HEADS * HEAD_PAD), lambda b, i, j: (b, i, 0)),
            pl.BlockSpec((1, tile, N_HEADS * HEAD_PAD), lambda b, i, j: (b, jnp.minimum(i, j), 0)),
            pl.BlockSpec((1, tile, KV_LORA_RANK), lambda b, i, j: (b, jnp.minimum(i, j), 0)),
            pl.BlockSpec((1, tile, D_MODEL), lambda b, i, j: (b, i, 0)),
            pl.BlockSpec((1, 1, D_MODEL), lambda b, i, j: (b, 0, 0)),
            full(w_uv), full(w_out),
        ],
        out_specs=pl.BlockSpec((1, tile, D_MODEL), lambda b, i, j: (b, i, 0)),
        out_shape=jax.ShapeDtypeStruct(x.shape, F32),
        scratch_shapes=[pltpu.VMEM((N_HEADS, tile, 1), F32), pltpu.VMEM((N_HEADS, tile, 1), F32),
                        pltpu.VMEM((N_HEADS, tile, KV_LORA_RANK), F32)],
        compiler_params=_cparams(3),
        name="prompt_attention",
    )(q, k, v, x, gate, w_uv, w_out)


PAGES_PER_STEP = 32
PAGES_PER_DOT = 4
QA_ROWS = 16


def _sample_attn_kernel(pt_ref, qa_ref, qr_ref, sself_ref, lat_new_ref, w_ukt_ref, *rest):
    del pt_ref
    pages = rest[:PAGES_PER_STEP]
    ctx_ref, m_s, l_s, acc_s, lat_s = rest[PAGES_PER_STEP:]
    step = pl.program_id(1)

    @pl.when(step == 0)
    def _():
        m_s[...] = sself_ref[0]
        l_s[...] = jnp.ones(l_s.shape, F32)
        acc_s[...] = jnp.broadcast_to(lat_new_ref[0], acc_s.shape)

    n_key_rows = N_HEADS * QK_NOPE_DIM
    n_dot = PAGES_PER_DOT * PAGE_SIZE
    lhs = jnp.concatenate([w_ukt_ref[...], qa_ref[0]], axis=0)
    qr = qr_ref[0]
    head_row = lax.broadcasted_iota(jnp.int32, (N_HEADS, n_dot), 0)
    scores = []
    for c in range(PAGES_PER_STEP // PAGES_PER_DOT):
        rows_t = jnp.concatenate([pages[c * PAGES_PER_DOT + p][...] for p in range(PAGES_PER_DOT)], axis=1)
        lat_t = rows_t[:KV_LORA_RANK].astype(BF16)
        kr_t = rows_t[KV_LORA_RANK:].astype(BF16)
        lat_s[:, c * n_dot:(c + 1) * n_dot] = lat_t
        yq = jnp.dot(lhs, lat_t, preferred_element_type=F32)
        ssq = jnp.zeros(head_row.shape, F32)
        for hd in range(N_HEADS):
            y = yq[hd * QK_NOPE_DIM:(hd + 1) * QK_NOPE_DIM]
            ssq = jnp.where(head_row == hd, jnp.sum(y * y, axis=0, keepdims=True), ssq)
        rinv = lax.rsqrt(ssq * (1.0 / QK_NOPE_DIM) + EPS)
        scores.append(yq[n_key_rows:n_key_rows + N_HEADS] * rinv
                      + jnp.dot(qr, kr_t, preferred_element_type=F32))
    s = jnp.concatenate(scores, axis=1)
    m_prev = m_s[...]
    m_new = jnp.maximum(m_prev, jnp.max(s, axis=-1, keepdims=True))
    alpha = jnp.exp2(m_prev - m_new)
    p = jnp.exp2(s - m_new)
    l_s[...] = alpha * l_s[...] + jnp.sum(p, axis=-1, keepdims=True)
    acc_s[...] = alpha * acc_s[...] + _bdot_nt(p, lat_s[...])
    m_s[...] = m_new

    @pl.when(step == pl.num_programs(1) - 1)
    def _():
        ctx_ref[0] = acc_s[...] / l_s[...]


def _sample_attention(cache_t, layer, page_table, qa, qr, sself, lat_new, w_ukt):
    nb, n_pages = page_table.shape
    n_steps = n_pages // PAGES_PER_STEP
    per_b = lambda a: pl.BlockSpec((1,) + a.shape[1:], lambda b, s, pt: (b,) + (0,) * (a.ndim - 1))
    full = lambda a: pl.BlockSpec(a.shape, lambda b, s, pt: (0,) * a.ndim)

    def page_spec(j):
        return pl.BlockSpec((None, None, CACHE_ROW, PAGE_SIZE),
                            lambda b, s, pt: (layer, pt[b * n_pages + s * PAGES_PER_STEP + j], 0, 0))

    grid_spec = pltpu.PrefetchScalarGridSpec(
        num_scalar_prefetch=1,
        grid=(nb, n_steps),
        in_specs=[per_b(qa), per_b(qr), per_b(sself), per_b(lat_new), full(w_ukt)]
        + [page_spec(j) for j in range(PAGES_PER_STEP)],
        out_specs=pl.BlockSpec((1, N_HEADS, KV_LORA_RANK), lambda b, s, pt: (b, 0, 0)),
        scratch_shapes=[pltpu.VMEM((N_HEADS, 1), F32), pltpu.VMEM((N_HEADS, 1), F32),
                        pltpu.VMEM((N_HEADS, KV_LORA_RANK), F32),
                        pltpu.VMEM((KV_LORA_RANK, PAGES_PER_STEP * PAGE_SIZE), BF16)],
    )
    return pl.pallas_call(
        _sample_attn_kernel,
        grid_spec=grid_spec,
        out_shape=jax.ShapeDtypeStruct((nb, N_HEADS, KV_LORA_RANK), F32),
        compiler_params=_cparams(2),
        name="sample_attention",
    )(page_table.reshape(-1), qa, qr, sself, lat_new, w_ukt, *([cache_t] * PAGES_PER_STEP))


def _sample_out_kernel(ctx_ref, x_ref, gate_ref, w_uv_ref, w_out_ref, o_ref):
    o_ref[...] = _attn_out(lambda hd: ctx_ref[:, hd * KV_LORA_RANK:(hd + 1) * KV_LORA_RANK],
                           x_ref[...], gate_ref[...], w_uv_ref, w_out_ref)


def _sample_out(ctx, x, gate, w_uv, w_out):
    return pl.pallas_call(
        _sample_out_kernel,
        out_shape=jax.ShapeDtypeStruct(x.shape, F32),
        compiler_params=pltpu.CompilerParams(vmem_limit_bytes=VMEM_LIMIT),
        name="sample_attention_out",
    )(ctx, x, gate, w_uv, w_out)


def _pool_mix(window_sum, h, cnt, w_pool_ref, pscale):
    parts = []
    for gi, w in enumerate(POOL_WINDOWS):
        ch = slice(gi * POOL_GROUP_DIM, (gi + 1) * POOL_GROUP_DIM)
        acc = h[:, ch]
        for kk in range(1, w):
            acc = acc + window_sum(kk, ch)
        d = acc / cnt[gi] - h[:, ch]
        parts.append(_bdot(d, w_pool_ref[gi]))
    return jnp.concatenate(parts, axis=-1) * pscale


def _pool_prompt_kernel(x_ref, shift_ref, scale_ref, gate_ref, gmix_ref, w_pool_ref, pscale_ref,
                        o_ref, st_ref, ext_s):
    si = pl.program_id(1)
    t = x_ref.shape[1]

    @pl.when(si == 0)
    def _():
        ext_s[0:POOL_HALO] = jnp.zeros((POOL_HALO, D_MODEL), F32)

    @pl.when(si > 0)
    def _():
        ext_s[0:POOL_HALO] = ext_s[t:t + POOL_HALO]

    x = x_ref[0]
    h = _rms(x, gmix_ref[...]) * (1.0 + scale_ref[0]) + shift_ref[0]
    ext_s[POOL_HALO:POOL_HALO + t] = h
    pos = si * t + lax.broadcasted_iota(jnp.int32, (t, 1), 0)
    cnt = [jnp.minimum(pos + 1, w).astype(F32) for w in POOL_WINDOWS]
    y = _pool_mix(lambda kk, ch: ext_s[POOL_HALO - kk:POOL_HALO - kk + t, ch], h, cnt,
                  w_pool_ref, pscale_ref[...])
    o_ref[0] = x + gate_ref[0] * y

    @pl.when(si == pl.num_programs(1) - 1)
    def _():
        st_ref[0] = ext_s[POOL_HALO + t - POOL_STATE_LEN:POOL_HALO + t]


def _pool_prompt(x, shift, scale, gate, gmix, w_pool, pscale, tile):
    nb, s, _ = x.shape
    tok = pl.BlockSpec((1, tile, D_MODEL), lambda b, i: (b, i, 0))
    mod = pl.BlockSpec((1, 1, D_MODEL), lambda b, i: (b, 0, 0))
    full = lambda a: pl.BlockSpec(a.shape, lambda b, i: (0,) * a.ndim)
    return pl.pallas_call(
        _pool_prompt_kernel,
        grid=(nb, s // tile),
        in_specs=[tok, mod, mod, mod, full(gmix), full(w_pool), full(pscale)],
        out_specs=[tok, pl.BlockSpec((1, POOL_STATE_LEN, D_MODEL), lambda b, i: (b, 0, 0))],
        out_shape=[jax.ShapeDtypeStruct(x.shape, F32),
                   jax.ShapeDtypeStruct((nb, POOL_STATE_LEN, D_MODEL), F32)],
        scratch_shapes=[pltpu.VMEM((POOL_HALO + tile, D_MODEL), F32)],
        compiler_params=_cparams(2),
        name="pool_prompt",
    )(x, shift, scale, gate, gmix, w_pool, pscale)


def _pool_sample_kernel(x_ref, shift_ref, scale_ref, gate_ref, gmix_ref, w_pool_ref, pscale_ref,
                        state_ref, o_ref, st_ref):
    x = x_ref[...]
    h = _rms(x, gmix_ref[...]) * (1.0 + scale_ref[...]) + shift_ref[...]
    cnt = [float(w) for w in POOL_WINDOWS]
    y = _pool_mix(lambda kk, ch: state_ref[POOL_STATE_LEN - kk, :, ch], h, cnt,
                  w_pool_ref, pscale_ref[...])
    o_ref[...] = x + gate_ref[...] * y
    for r in range(POOL_STATE_LEN - 1):
        st_ref[r] = state_ref[r + 1]
    st_ref[POOL_STATE_LEN - 1] = h


def _pool_sample(x, shift, scale, gate, gmix, w_pool, pscale, state_t):
    return pl.pallas_call(
        _pool_sample_kernel,
        out_shape=[jax.ShapeDtypeStruct(x.shape, F32), jax.ShapeDtypeStruct(state_t.shape, F32)],
        compiler_params=pltpu.CompilerParams(vmem_limit_bytes=VMEM_LIMIT),
        name="pool_sample",
    )(x, shift, scale, gate, gmix, w_pool, pscale, state_t)


NOT_TOP = 64.0
LANES = 128
SUBLANES = 8


def _sort_pairs(n):
    pairs, p = [], 1
    while p < n:
        k = p
        while k >= 1:
            for j in range(k % p, n - k, 2 * k):
                for i in range(min(k, n - j - k)):
                    if (i + j) // (2 * p) == (i + j + k) // (2 * p):
                        pairs.append((i + j, i + j + k))
            k //= 2
        p *= 2
    return pairs


def _bitonic_merge_pairs(n):
    pairs, k = [], n // 2
    while k >= 1:
        pairs += [(i, i + k) for i in range(n) if not i & k]
        k //= 2
    return pairs


def _compare_exchange(v, pairs):
    for i, j in pairs:
        v[i], v[j] = jnp.maximum(v[i], v[j]), jnp.minimum(v[i], v[j])
    return v


def _top16_sorted(s):
    v = [s[r * SUBLANES:(r + 1) * SUBLANES] for r in range(PEER_TOPK)]
    v = _compare_exchange(v, _sort_pairs(PEER_TOPK))
    for shift in (4, 2, 1):
        v = [jnp.maximum(v[i], pltpu.roll(v[PEER_TOPK - 1 - i], shift, axis=0)) for i in range(PEER_TOPK)]
        v = _compare_exchange(v, _bitonic_merge_pairs(PEER_TOPK))
    return v


def _kth_largest(work, k):
    for _ in range(k):
        m = jnp.max(work, axis=0, keepdims=True)
        work = jnp.where(work == m, NEG_INF, work)
    return m


def _prefix_count(x, thresholds, above):
    th = thresholds
    parts = []
    for r in range(x.shape[0] // SUBLANES):
        xr = x[r * SUBLANES:(r + 1) * SUBLANES]
        passes = (lambda t: t > xr) if above else (lambda t: xr >= t)
        c8 = passes(th[7])
        c4 = passes(jnp.where(c8, th[11], th[3]))
        c2 = passes(jnp.where(c8, jnp.where(c4, th[13], th[9]), jnp.where(c4, th[5], th[1])))
        c1 = passes(jnp.where(c8, jnp.where(c4, jnp.where(c2, th[14], th[12]), jnp.where(c2, th[10], th[8])),
                              jnp.where(c4, jnp.where(c2, th[6], th[4]), jnp.where(c2, th[2], th[0]))))
        cnt = (jnp.where(c8, 8.0, 0.0) + jnp.where(c4, 4.0, 0.0)) + (jnp.where(c2, 2.0, 0.0) + jnp.where(c1, 1.0, 0.0))
        parts.append(jnp.where(passes(th[15]), float(PEER_TOPK), cnt))
    return jnp.concatenate(parts, axis=0)


def _dup_bf16(x):
    hi = pltpu.bitcast(x.astype(BF16).astype(F32), jnp.uint32)
    return hi | (hi >> 16)


def _pack_pairs(x):
    return pltpu.bitcast(x.astype(BF16), jnp.uint32)


def _peer_route_kernel(x_ref, shift_ref, scale_ref, gffn_ref, wqt_ref, subk_ref,
                       ht_ref, r2_ref, e2_ref, n1_ref, e1_ref):
    t = x_ref.shape[1]
    h = _rms(x_ref[0], gffn_ref[...]) * (1.0 + scale_ref[0]) + shift_ref[0]
    ht = h.T.astype(BF16)
    ht_ref[...] = pltpu.bitcast(ht, jnp.uint32)
    qt = jnp.dot(wqt_ref[...], ht, preferred_element_type=F32).astype(BF16)
    row8 = lax.broadcasted_iota(jnp.int32, (SUBLANES, LANES), 0)
    for hd in range(PEER_HEADS):
        sc = [jnp.dot(subk_ref[2 * hd + p], qt[(2 * hd + p) * PEER_HALF:(2 * hd + p + 1) * PEER_HALF],
                      preferred_element_type=F32) for p in range(2)]
        for lt in range(t // LANES):
            ln = slice(lt * LANES, (lt + 1) * LANES)
            s1, s2 = sc[0][:, ln], sc[1][:, ln]
            top1, top2 = _top16_sorted(s1), _top16_sorted(s2)
            v1_lo, v1_hi = top1[0], top1[SUBLANES]
            for a in range(1, SUBLANES):
                v1_lo = jnp.where(row8 == a, top1[a], v1_lo)
                v1_hi = jnp.where(row8 == a, top1[SUBLANES + a], v1_hi)
            t1, t2 = top1, top2
            below = lambda r: row8 < r
            cand = jnp.concatenate([
                v1_lo + t2[0],
                v1_hi + t2[0],
                v1_lo + t2[1],
                jnp.where(below(5), v1_lo, jnp.where(row8 == 5, t1[0], jnp.where(row8 == 6, t1[1], t1[2])))
                + jnp.where(below(5), t2[2], t2[4]),
                jnp.where(below(4), v1_lo, jnp.where((row8 & 1) == 0, t1[0], t1[1]))
                + jnp.where(below(4), t2[3], jnp.where(below(6), t2[5], t2[6])),
                jnp.where(below(2), v1_lo, t1[0])
                + jnp.where(below(2), t2[7], jnp.where(below(3), t2[8], jnp.where(below(4), t2[9], jnp.where(
                    below(5), t2[10], jnp.where(below(6), t2[11], jnp.where(below(7), t2[12], t2[13])))))),
                t1[0] + jnp.where(below(1), t2[14], jnp.where(below(2), t2[15], NEG_INF)),
            ], axis=0)
            tau = _kth_largest(cand, PEER_TOPK)
            cmax = top1[0] + top2[0]
            z = jnp.sum(jnp.where(cand >= tau, jnp.exp(cand - cmax[:1]), 0.0), axis=0, keepdims=True)
            theta = []
            for b in range(PEER_TOPK):
                lo = jnp.where(v1_lo + top2[b] >= tau, v1_lo, -NEG_INF)
                hi = jnp.where(v1_hi + top2[b] >= tau, v1_hi, -NEG_INF)
                theta.append(jnp.min(jnp.minimum(lo, hi), axis=0, keepdims=True))
            n1 = _prefix_count(s1, theta, above=False)
            rank2 = _prefix_count(s2, top2, above=True)
            rank2 = jnp.where(rank2 >= PEER_TOPK, NOT_TOP, rank2)
            r2_ref[hd, :, ln] = _pack_pairs(rank2)
            e2_ref[hd, :, ln] = _pack_pairs(jnp.exp(s2 - top2[0][:1]))
            n1_ref[hd, lt] = _dup_bf16(n1)
            e1_ref[hd, lt] = _dup_bf16(jnp.exp(s1 - top1[0][:1]) * (0.5 / z))


def _peer_route(x, shift, scale, gffn, wqt, subk, tile):
    nb, s, _ = x.shape
    r = shift.shape[1]
    ns = s // tile
    n = nb * s
    rt = tile if r == s else 1
    tok = pl.BlockSpec((1, tile, D_MODEL), lambda b, i: (b, i, 0))
    mod = pl.BlockSpec((1, rt, D_MODEL), (lambda b, i: (b, i, 0)) if r == s else (lambda b, i: (b, 0, 0)))
    full = lambda a: pl.BlockSpec(a.shape, lambda b, i: (0,) * a.ndim)
    u32 = jnp.uint32
    lane_tiles = tile // LANES
    pairs = pl.BlockSpec((PEER_HEADS, PEER_N_KEYS // 2, tile), lambda b, i: (0, 0, b * ns + i))
    rows = pl.BlockSpec((PEER_HEADS, lane_tiles, PEER_N_KEYS, LANES), lambda b, i: (0, b * ns + i, 0, 0))
    pairs_shape = jax.ShapeDtypeStruct((PEER_HEADS, PEER_N_KEYS // 2, n), u32)
    rows_shape = jax.ShapeDtypeStruct((PEER_HEADS, n // LANES, PEER_N_KEYS, LANES), u32)
    return pl.pallas_call(
        _peer_route_kernel,
        grid=(nb, ns),
        in_specs=[tok, mod, mod, full(gffn), full(wqt), full(subk)],
        out_specs=[pl.BlockSpec((D_MODEL // 2, tile), lambda b, i: (0, b * ns + i)),
                   pairs, pairs, rows, rows],
        out_shape=[jax.ShapeDtypeStruct((D_MODEL // 2, n), u32), pairs_shape, pairs_shape,
                   rows_shape, rows_shape],
        compiler_params=_cparams(2),
        name="peer_route",
    )(x, shift, scale, gffn, wqt, subk)


N_CHUNKS = PEER_N_EXPERTS // EXPERT_CHUNK
GATE_ROWS = 32
OUT_DEPTH = 1024
OUT_ROWS = 1024
KEYS_PER_PIECE = 8


def _peer_sweep_kernel(ht_ref, r2_ref, e2_ref, n1_ref, e1_ref, u_ref, vt_ref, x_ref, gate_ref, o_ref,
                       ga_s, yt_s):
    j = pl.program_id(2)
    t = ht_ref.shape[1]

    @pl.when(j == 0)
    def _():
        yt_s[...] = jnp.zeros(yt_s.shape, F32)

    ht = pltpu.bitcast(ht_ref[...], BF16)
    zero = jnp.zeros((GATE_ROWS, LANES), BF16)
    n_groups = PEER_N_KEYS // GATE_ROWS

    def gate_piece(k):
        n_rows = KEYS_PER_PIECE * PEER_N_KEYS
        u = pltpu.bitcast(u_ref[k * n_rows // 2:(k + 1) * n_rows // 2, :], BF16)
        act = jnp.dot(u, ht, preferred_element_type=F32)
        ga_s[k * n_rows:(k + 1) * n_rows, :] = (act * (1.0 + lax.erf(act * (2.0 ** -0.5)))).astype(BF16)
        for lt in range(t // LANES):
            ln = slice(lt * LANES, (lt + 1) * LANES)
            g = [[zero] * n_groups for _ in range(KEYS_PER_PIECE)]
            for hd in range(PEER_HEADS):
                n1, e1 = [], []
                for kk in range(KEYS_PER_PIECE):
                    row = pl.ds(j * KEY_ROWS_PER_CHUNK + k * KEYS_PER_PIECE + kk, 1)
                    spread = lambda ref: pltpu.bitcast(
                        jnp.broadcast_to(ref[hd, lt, row, :], (GATE_ROWS // 2, LANES)), BF16)
                    n1.append(spread(n1_ref))
                    e1.append(spread(e1_ref))
                for sg in range(n_groups):
                    pairs = slice(sg * GATE_ROWS // 2, (sg + 1) * GATE_ROWS // 2)
                    r2 = pltpu.bitcast(r2_ref[hd, pairs, ln], BF16)
                    e2 = pltpu.bitcast(e2_ref[hd, pairs, ln], BF16)
                    for kk in range(KEYS_PER_PIECE):
                        g[kk][sg] = g[kk][sg] + jnp.where(r2 < n1[kk], e2 * e1[kk], zero)
            for kk in range(KEYS_PER_PIECE):
                for sg in range(n_groups):
                    r0 = k * n_rows + kk * PEER_N_KEYS + sg * GATE_ROWS
                    ga_s[r0:r0 + GATE_ROWS, ln] = ga_s[r0:r0 + GATE_ROWS, ln] * g[kk][sg]

    def out_piece(group, m):
        cols = slice(group * OUT_DEPTH, (group + 1) * OUT_DEPTH)
        rows = slice(m * OUT_ROWS, (m + 1) * OUT_ROWS)
        v = pltpu.bitcast(vt_ref[m * OUT_ROWS // 2:(m + 1) * OUT_ROWS // 2, cols], BF16)
        yt_s[rows, :] += jnp.dot(v, ga_s[cols, :], preferred_element_type=F32)

    pieces_per_group = OUT_DEPTH // (KEYS_PER_PIECE * PEER_N_KEYS)
    n_pieces = KEY_ROWS_PER_CHUNK // KEYS_PER_PIECE
    pending = []
    for k in range(n_pieces):
        gate_piece(k)
        if pending:
            out_piece(*pending.pop(0))
        if (k + 1) % pieces_per_group == 0:
            pending += [(k // pieces_per_group, m) for m in range(D_MODEL // OUT_ROWS)]
    for piece in pending:
        out_piece(*piece)

    @pl.when(j == pl.num_programs(2) - 1)
    def _():
        o_ref[0] = x_ref[0] + gate_ref[0] * yt_s[...].T


def _peer_sweep(ht, r2, e2, n1, e1, u, vt, x, gate, tile):
    nb, s, _ = x.shape
    r = gate.shape[1]
    ns = s // tile
    pairs = pl.BlockSpec((PEER_HEADS, PEER_N_KEYS // 2, tile), lambda b, i, j: (0, 0, b * ns + i))
    rows = pl.BlockSpec((PEER_HEADS, tile // LANES, PEER_N_KEYS, LANES),
                        lambda b, i, j: (0, b * ns + i, 0, 0))
    tok = pl.BlockSpec((1, tile, D_MODEL), lambda b, i, j: (b, i, 0))
    mod = pl.BlockSpec((1, tile if r == s else 1, D_MODEL),
                       (lambda b, i, j: (b, i, 0)) if r == s else (lambda b, i, j: (b, 0, 0)))
    return pl.pallas_call(
        _peer_sweep_kernel,
        grid=(nb, ns, N_CHUNKS),
        in_specs=[pl.BlockSpec((D_MODEL // 2, tile), lambda b, i, j: (0, b * ns + i)),
                  pairs, pairs, rows, rows,
                  pl.BlockSpec((EXPERT_CHUNK // 2, D_MODEL), lambda b, i, j: (j, 0)),
                  pl.BlockSpec((D_MODEL // 2, EXPERT_CHUNK), lambda b, i, j: (0, j)),
                  tok, mod],
        out_specs=tok,
        out_shape=jax.ShapeDtypeStruct(x.shape, F32),
        scratch_shapes=[pltpu.VMEM((EXPERT_CHUNK, tile), BF16), pltpu.VMEM((D_MODEL, tile), F32)],
        compiler_params=_cparams(3),
        name="peer_sweep",
    )(ht, r2, e2, n1, e1, u, vt, x, gate)


def _peer(x, shift, scale, gate, gffn, pw, route_tile, sweep_tile):
    routed = _peer_route(x, shift, scale, gffn, pw["wqt"], pw["subk"], route_tile)
    return _peer_sweep(*routed, pw["u"], pw["vt"], x, gate, sweep_tile)


def _rope_tables(pos):
    inv = ROPE_THETA ** (-jnp.arange(0, QK_ROPE_DIM, 2, dtype=F32) / QK_ROPE_DIM)
    ang = pos.astype(F32)[:, None] * inv[None, :]
    c, s, z = jnp.cos(ang), jnp.sin(ang), jnp.zeros_like(ang)
    return jnp.concatenate([c, z, c, z], -1), jnp.concatenate([-s, z, s, z], -1)


def _pad_rope_cols(w):
    z = jnp.zeros(w.shape[:-1] + (ROPE_HALF,), w.dtype)
    return jnp.concatenate([w[..., :ROPE_HALF], z, w[..., ROPE_HALF:], z], -1)


def _mla_weights(w_in, g_q_lat, g_kv_lat, w_uq, w_uk, w_uv, g_qn_nope, g_qn_rope, g_kn_nope,
                 g_kn_rope, w_out):
    lat_end = Q_LORA_RANK + KV_LORA_RANK
    w_in_p = jnp.concatenate([w_in[:, :lat_end], _pad_rope_cols(w_in[:, lat_end:])], -1)
    wq = w_uq.reshape(Q_LORA_RANK, N_HEADS, QK_NOPE_DIM + QK_ROPE_DIM)
    wq = jnp.concatenate([wq[..., :QK_NOPE_DIM], _pad_rope_cols(wq[..., QK_NOPE_DIM:])], -1)
    row = lambda g: g.reshape(1, -1)
    return {
        "w_in": w_in_p.astype(BF16),
        "g_q_lat": row(g_q_lat), "g_kv_lat": row(g_kv_lat), "g_kr": row(_pad_rope_cols(g_kn_rope)),
        "w_uq": wq.reshape(Q_LORA_RANK, N_HEADS * HEAD_PAD).astype(BF16),
        "g_qn": row(g_qn_nope), "g_qr": row(_pad_rope_cols(g_qn_rope)),
        "w_uk": w_uk.reshape(KV_LORA_RANK, N_HEADS * QK_NOPE_DIM).astype(BF16),
        "g_kn": row(g_kn_nope),
        "w_ukt": jnp.transpose(w_uk, (1, 2, 0)).astype(BF16),
        "w_uv": jnp.transpose(w_uv, (1, 0, 2)).astype(BF16),
        "w_out": w_out.astype(BF16),
    }


def _pack_rows_kernel(x_ref, o_ref):
    o_ref[...] = _pack_pairs(x_ref[...])


def _pack_cols_kernel(x_ref, o_ref):
    o_ref[...] = _pack_pairs(x_ref[...].T)


def _pack_weight(w, layer, transpose):
    _, r, c = w.shape
    blk = 1024
    src = pl.BlockSpec((None, blk, c), lambda i: (layer, i, 0))
    if transpose:
        return pl.pallas_call(
            _pack_cols_kernel, grid=(r // blk,), in_specs=[src],
            out_specs=pl.BlockSpec((c // 2, blk), lambda i: (0, i)),
            out_shape=jax.ShapeDtypeStruct((c // 2, r), jnp.uint32),
            compiler_params=_cparams(1), name="pack_weight_t")(w)
    return pl.pallas_call(
        _pack_rows_kernel, grid=(r // blk,), in_specs=[src],
        out_specs=pl.BlockSpec((blk // 2, c), lambda i: (i, 0)),
        out_shape=jax.ShapeDtypeStruct((r // 2, c), jnp.uint32),
        compiler_params=_cparams(1), name="pack_weight")(w)


def _peer_weights(w_q, subkeys, u, v, layer):
    return {
        "wqt": w_q[layer].T.astype(BF16),
        "subk": subkeys[layer].reshape(2 * PEER_HEADS, PEER_N_KEYS, PEER_HALF).astype(BF16),
        "u": _pack_weight(u, layer, False),
        "vt": _pack_weight(v, layer, True),
    }


PROMPT_TILE = 256
SAMPLE_TILE = 128
PEER_SWEEP_TILE = 512
ATTN_TILE = 512
Q_SUB = 256


def kernel(x_prompt, x_sample, c_prompt, c_sample, cache_mla, state_pool, page_table, norm_mix, norm_ffn, w_ada, b_ada, w_mla_in, g_q_lat, g_kv_lat, w_uq, w_uk, w_uv, g_qn_nope, g_qn_rope, g_kn_nope, g_kn_rope, w_mla_out, w_pool, pool_scale, w_peer_q, peer_subkeys, peer_u, peer_v):
    nbp, seq, _ = x_prompt.shape
    nbs = x_sample.shape[0]
    depth = w_ada.shape[0]
    xp = x_prompt
    xs = x_sample.reshape(1, nbs, D_MODEL)

    mods = _ada_params(jnp.concatenate([c_prompt, c_sample], 0), w_ada, b_ada)
    mods = mods.reshape(depth, nbp + nbs, 6, D_MODEL)
    cs_p, sn_p = _rope_tables(jnp.arange(seq))
    cs_s, sn_s = _rope_tables(jnp.full((nbs,), PAST_LEN))
    cache_t = jnp.swapaxes(cache_mla, 2, 3)

    rows_p, rows_s, pst_p, pst_s = [], [], [], []
    for l in range(depth):
        mp = [mods[l, :nbp, k].reshape(nbp, 1, D_MODEL) for k in range(6)]
        ms = [mods[l, nbp:, k].reshape(1, nbs, D_MODEL) for k in range(6)]
        gmix = norm_mix[l].reshape(1, D_MODEL)
        gffn = norm_ffn[l].reshape(1, D_MODEL)
        i = l // 2
        if l % 2 == 0:
            mw = _mla_weights(w_mla_in[i], g_q_lat[i], g_kv_lat[i], w_uq[i], w_uk[i], w_uv[i],
                              g_qn_nope[i], g_qn_rope[i], g_kn_nope[i], g_kn_rope[i], w_mla_out[i])
            r_p, q, k, v = _mla_project(xp, mp[0], mp[1], gmix, cs_p, sn_p, mw, PROMPT_TILE, False)
            xp = _prompt_attention(q, k, v, xp, mp[2], mw["w_uv"], mw["w_out"], ATTN_TILE)
            r_s, q, k, v, qa, sself = _mla_project(xs, ms[0], ms[1], gmix, cs_s, sn_s, mw,
                                                   SAMPLE_TILE, True)
            q = q.reshape(nbs, N_HEADS, HEAD_PAD)[:, :, QK_NOPE_DIM:]
            qr = jnp.concatenate([q[..., :ROPE_HALF], q[..., 2 * ROPE_HALF:3 * ROPE_HALF]], -1)
            qa = jnp.pad(qa.reshape(nbs, N_HEADS, KV_LORA_RANK), ((0, 0), (0, QA_ROWS - N_HEADS), (0, 0)))
            ctx = _sample_attention(
                cache_t, i, page_table, qa, qr, sself.reshape(nbs, 128)[:, :N_HEADS, None],
                r_s.reshape(nbs, 1, CACHE_ROW)[..., :KV_LORA_RANK],
                mw["w_ukt"].reshape(N_HEADS * QK_NOPE_DIM, KV_LORA_RANK))
            xs = _sample_out(ctx.reshape(nbs, N_HEADS * KV_LORA_RANK), xs[0], ms[2][0],
                             mw["w_uv"], mw["w_out"]).reshape(1, nbs, D_MODEL)
            rows_p.append(r_p)
            rows_s.append(r_s.reshape(nbs, 1, CACHE_ROW))
        else:
            wp = w_pool[i].astype(BF16)
            psc = pool_scale[i].reshape(1, D_MODEL)
            xp, st_p = _pool_prompt(xp, mp[0], mp[1], mp[2], gmix, wp, psc, PROMPT_TILE)
            x2, st_s = _pool_sample(xs[0], ms[0][0], ms[1][0], ms[2][0], gmix, wp, psc,
                                    jnp.transpose(state_pool[i], (1, 0, 2)))
            xs = x2.reshape(1, nbs, D_MODEL)
            pst_p.append(st_p)
            pst_s.append(jnp.transpose(st_s, (1, 0, 2)))
        pw = _peer_weights(w_peer_q, peer_subkeys, peer_u, peer_v, l)
        xp = _peer(xp, mp[3], mp[4], mp[5], gffn, pw, PROMPT_TILE, PEER_SWEEP_TILE)
        xs = _peer(xs, ms[3], ms[4], ms[5], gffn, pw, SAMPLE_TILE, SAMPLE_TILE)
    return (xp, xs.reshape(nbs, 1, D_MODEL), jnp.stack(rows_p), jnp.stack(rows_s),
            jnp.stack(pst_p), jnp.stack(pst_s))
```

```python
import functools

import jax
import jax.numpy as jnp
from jax import lax
from jax.experimental import pallas as pl
from jax.experimental.pallas import tpu as pltpu

F32 = jnp.float32
BF16 = jnp.bfloat16

D_MODEL = 1024
EPS = 1e-6
PAST_LEN = 8192
PAGE_SIZE = 128
N_HEADS = 8
QK_NOPE_DIM = 128
QK_ROPE_DIM = 64
ROPE_HALF = QK_ROPE_DIM // 2
ROPE_PAD = 128
HEAD_PAD = QK_NOPE_DIM + ROPE_PAD
V_HEAD_DIM = 128
Q_LORA_RANK = 384
KV_LORA_RANK = 256
CACHE_ROW = KV_LORA_RANK + QK_ROPE_DIM
ROPE_THETA = 10000.0
ATTN_SCALE = (QK_NOPE_DIM + QK_ROPE_DIM) ** -0.5
LOG2_E = 1.4426950408889634
POOL_WINDOWS = (2, 4, 8, 16)
POOL_GROUP_DIM = D_MODEL // len(POOL_WINDOWS)
POOL_STATE_LEN = max(POOL_WINDOWS) - 1
POOL_HALO = 16
PEER_HEADS = 8
PEER_N_KEYS = 128
PEER_N_EXPERTS = PEER_N_KEYS * PEER_N_KEYS
PEER_TOPK = 16
PEER_HALF = 128
EXPERT_CHUNK = 2048
KEY_ROWS_PER_CHUNK = EXPERT_CHUNK // PEER_N_KEYS

VMEM_LIMIT = 56 * 1024 * 1024

NEG_INF = float("-inf")


def _cparams(n_grid):
    return pltpu.CompilerParams(
        dimension_semantics=("arbitrary",) * n_grid, vmem_limit_bytes=VMEM_LIMIT)


def _rms(x, g, n=None):
    n = x.shape[-1] if n is None else n
    ms = jnp.sum(x * x, axis=-1, keepdims=True) * (1.0 / n)
    return x * lax.rsqrt(ms + EPS) * g


def _bdot(a, b):
    return jnp.dot(a.astype(BF16), b.astype(BF16), preferred_element_type=F32)


def _bdot_nt(a, b):
    return lax.dot_general(a.astype(BF16), b.astype(BF16), (((1,), (1,)), ((), ())),
                           preferred_element_type=F32)


def _mods_kernel(c_ref, w_ref, b_ref, o_ref):
    o_ref[0] = _bdot(jax.nn.silu(c_ref[...]), w_ref[0]) + b_ref[0]


def _ada_params(c_all, w_ada, b_ada):
    n_layers = w_ada.shape[0]
    n_rows = c_all.shape[0]
    n_col = w_ada.shape[2] // D_MODEL
    return pl.pallas_call(
        _mods_kernel,
        grid=(n_layers, n_col),
        in_specs=[
            pl.BlockSpec((n_rows, D_MODEL), lambda l, k: (0, 0)),
            pl.BlockSpec((1, D_MODEL, D_MODEL), lambda l, k: (l, 0, k)),
            pl.BlockSpec((1, 1, D_MODEL), lambda l, k: (l, 0, k)),
        ],
        out_specs=pl.BlockSpec((1, n_rows, D_MODEL), lambda l, k: (l, 0, k)),
        out_shape=jax.ShapeDtypeStruct((n_layers, n_rows, n_col * D_MODEL), F32),
        compiler_params=_cparams(2),
        name="ada_params",
    )(c_all, w_ada, b_ada.reshape(n_layers, 1, -1))


def _rope_pad(x, g, cs, sn):
    y = _rms(x, g, QK_ROPE_DIM)
    return y * cs + pltpu.roll(y, ROPE_PAD // 2, axis=1) * sn


def _mla_proj_kernel(absorb, x_ref, shift_ref, scale_ref, gmix_ref, cs_ref, sn_ref, w_in_ref,
                     g_q_lat_ref, g_kv_lat_ref, g_kr_ref, w_uq_ref, g_qn_ref, g_qr_ref,
                     w_uk_ref, g_kn_ref, *rest):
    if absorb:
        w_ukt_ref, rows_ref, q_ref, k_ref, v_ref, qa_ref, sself_ref = rest
    else:
        rows_ref, q_ref, k_ref, v_ref = rest
    cs = cs_ref[...]
    sn = sn_ref[...]
    h = _rms(x_ref[0], gmix_ref[...]) * (1.0 + scale_ref[0]) + shift_ref[0]
    z = _bdot(h, w_in_ref[...])
    cq = _rms(z[:, :Q_LORA_RANK], g_q_lat_ref[...])
    lat = _rms(z[:, Q_LORA_RANK:Q_LORA_RANK + KV_LORA_RANK], g_kv_lat_ref[...])
    kr = _rope_pad(z[:, Q_LORA_RANK + KV_LORA_RANK:], g_kr_ref[...], cs, sn)
    rows_ref[0, :, :KV_LORA_RANK] = lat
    rows_ref[0, :, KV_LORA_RANK:] = (kr + pltpu.roll(kr, ROPE_PAD - ROPE_HALF, axis=1))[:, :QK_ROPE_DIM]
    v_ref[0] = lat.astype(BF16)
    q = _bdot(cq, w_uq_ref[...])
    kn_all = _bdot(lat, w_uk_ref[...])
    kr_b = kr.astype(BF16)
    if absorb:
        lane = lax.broadcasted_iota(jnp.int32, (x_ref.shape[1], 128), 1)
        sself = jnp.zeros((x_ref.shape[1], 128), F32)
    for hd in range(N_HEADS):
        o = hd * HEAD_PAD
        qn = _rms(q[:, o:o + QK_NOPE_DIM], g_qn_ref[...]) * (ATTN_SCALE * LOG2_E)
        qr = _rope_pad(q[:, o + QK_NOPE_DIM:o + HEAD_PAD], g_qr_ref[...], cs, sn) * (ATTN_SCALE * LOG2_E)
        kn = _rms(kn_all[:, hd * QK_NOPE_DIM:(hd + 1) * QK_NOPE_DIM], g_kn_ref[...])
        qn_b, qr_b, kn_b = qn.astype(BF16), qr.astype(BF16), kn.astype(BF16)
        q_ref[0, :, o:o + QK_NOPE_DIM] = qn_b
        q_ref[0, :, o + QK_NOPE_DIM:o + HEAD_PAD] = qr_b
        k_ref[0, :, o:o + QK_NOPE_DIM] = kn_b
        k_ref[0, :, o + QK_NOPE_DIM:o + HEAD_PAD] = kr_b
        if absorb:
            s = (jnp.sum(qn_b.astype(F32) * kn_b.astype(F32), axis=-1, keepdims=True)
                 + jnp.sum(qr_b.astype(F32) * kr_b.astype(F32), axis=-1, keepdims=True))
            sself = jnp.where(lane == hd, s, sself)
            qa = _bdot(qn * g_kn_ref[...], w_ukt_ref[hd])
            qa_ref[0, :, hd * KV_LORA_RANK:(hd + 1) * KV_LORA_RANK] = qa.astype(BF16)
    if absorb:
        sself_ref[0] = sself


def _mla_project(x, shift, scale, gmix, cs, sn, mw, tile, absorb):
    nb, s, _ = x.shape
    r = shift.shape[1]
    ns = s // tile
    rt = tile if r == s else 1
    tok = lambda bs: pl.BlockSpec(bs, lambda b, i: (b, i, 0))
    mod = pl.BlockSpec((1, rt, D_MODEL), (lambda b, i: (b, i, 0)) if r == s else (lambda b, i: (b, 0, 0)))
    full = lambda a: pl.BlockSpec(a.shape, lambda b, i: (0,) * a.ndim)
    weights = [mw["w_in"], mw["g_q_lat"], mw["g_kv_lat"], mw["g_kr"], mw["w_uq"], mw["g_qn"],
               mw["g_qr"], mw["w_uk"], mw["g_kn"]]
    if absorb:
        weights.append(mw["w_ukt"])
    out_shape = [jax.ShapeDtypeStruct((nb, s, CACHE_ROW), F32),
                 jax.ShapeDtypeStruct((nb, s, N_HEADS * HEAD_PAD), BF16),
                 jax.ShapeDtypeStruct((nb, s, N_HEADS * HEAD_PAD), BF16),
                 jax.ShapeDtypeStruct((nb, s, KV_LORA_RANK), BF16)]
    out_specs = [tok((1, tile, CACHE_ROW)), tok((1, tile, N_HEADS * HEAD_PAD)),
                 tok((1, tile, N_HEADS * HEAD_PAD)), tok((1, tile, KV_LORA_RANK))]
    if absorb:
        out_shape += [jax.ShapeDtypeStruct((nb, s, N_HEADS * KV_LORA_RANK), BF16),
                      jax.ShapeDtypeStruct((nb, s, 128), F32)]
        out_specs += [tok((1, tile, N_HEADS * KV_LORA_RANK)), tok((1, tile, 128))]
    return pl.pallas_call(
        functools.partial(_mla_proj_kernel, absorb),
        grid=(nb, ns),
        in_specs=[tok((1, tile, D_MODEL)), mod, mod, full(gmix),
                  pl.BlockSpec((tile, ROPE_PAD), lambda b, i: (i, 0)),
                  pl.BlockSpec((tile, ROPE_PAD), lambda b, i: (i, 0))] + [full(w) for w in weights],
        out_specs=out_specs,
        out_shape=out_shape,
        compiler_params=_cparams(2),
        name="mla_project_sample" if absorb else "mla_project",
    )(x, shift, scale, gmix, cs, sn, *weights)


def _attn_out(ctx_of_head, x, gate, w_uv_ref, w_out_ref):
    o = jnp.concatenate([_bdot(ctx_of_head(hd), w_uv_ref[hd]) for hd in range(N_HEADS)], axis=-1)
    return x + gate * _bdot(o, w_out_ref[...])


def _flash_kernel(q_ref, k_ref, v_ref, x_ref, gate_ref, w_uv_ref, w_out_ref, o_ref,
                  m_s, l_s, acc_s):
    qi = pl.program_id(1)
    ki = pl.program_id(2)
    tq = q_ref.shape[1]

    @pl.when(ki == 0)
    def _():
        m_s[...] = jnp.full(m_s.shape, NEG_INF, F32)
        l_s[...] = jnp.zeros(l_s.shape, F32)
        acc_s[...] = jnp.zeros(acc_s.shape, F32)

    def step(masked):
        v = v_ref[0]
        q_sub = min(Q_SUB, tq)
        for qs in range(tq // q_sub):
            rows = slice(qs * q_sub, (qs + 1) * q_sub)
            if masked:
                row = lax.broadcasted_iota(jnp.int32, (q_sub, tq), 0) + qs * q_sub
                col = lax.broadcasted_iota(jnp.int32, (q_sub, tq), 1)
                keep = col <= row
            for hd in range(N_HEADS):
                o = hd * HEAD_PAD
                s = _bdot_nt(q_ref[0, rows, o:o + HEAD_PAD], k_ref[0, :, o:o + HEAD_PAD])
                if masked:
                    s = jnp.where(keep, s, NEG_INF)
                m_prev = m_s[hd, rows]
                m_new = jnp.maximum(m_prev, jnp.max(s, axis=-1, keepdims=True))
                alpha = jnp.exp2(m_prev - m_new)
                p = jnp.exp2(s - m_new)
                l_s[hd, rows] = alpha * l_s[hd, rows] + jnp.sum(p, axis=-1, keepdims=True)
                acc_s[hd, rows] = alpha * acc_s[hd, rows] + _bdot(p, v)
                m_s[hd, rows] = m_new

    @pl.when(ki < qi)
    def _():
        step(False)

    @pl.when(ki == qi)
    def _():
        step(True)
        o_ref[0] = _attn_out(lambda hd: acc_s[hd] / l_s[hd], x_ref[0], gate_ref[0],
                             w_uv_ref, w_out_ref)


def _prompt_attention(q, k, v, x, gate, w_uv, w_out, tile):
    nb, s, _ = x.shape
    nt = s // tile
    full = lambda a: pl.BlockSpec(a.shape, lambda b, i, j: (0,) * a.ndim)
    return pl.pallas_call(
        _flash_kernel,
        grid=(nb, nt, nt),
        in_specs=[
            pl.BlockSpec((1, tile, N_HEADS * HEAD_PAD), lambda b, i, j: (b, i, 0)),
            pl.BlockSpec((1, tile, N_HEADS * HEAD_PAD), lambda b, i, j: (b, jnp.minimum(i, j), 0)),
            pl.BlockSpec((1, tile, KV_LORA_RANK), lambda b, i, j: (b, jnp.minimum(i, j), 0)),
            pl.BlockSpec((1, tile, D_MODEL), lambda b, i, j: (b, i, 0)),
            pl.BlockSpec((1, 1, D_MODEL), lambda b, i, j: (b, 0, 0)),
            full(w_uv), full(w_out),
        ],
        out_specs=pl.BlockSpec((1, tile, D_MODEL), lambda b, i, j: (b, i, 0)),
        out_shape=jax.ShapeDtypeStruct(x.shape, F32),
        scratch_shapes=[pltpu.VMEM((N_HEADS, tile, 1), F32), pltpu.VMEM((N_HEADS, tile, 1), F32),
                        pltpu.VMEM((N_HEADS, tile, KV_LORA_RANK), F32)],
        compiler_params=_cparams(3),
        name="prompt_attention",
    )(q, k, v, x, gate, w_uv, w_out)


PAGES_PER_STEP = 64
PAGES_PER_DOT = 4
QA_ROWS = 16


def _sample_attn_kernel(pt_ref, qa_ref, qr_ref, sself_ref, lat_new_ref, w_ukt_ref, *rest):
    del pt_ref
    pages = rest[:PAGES_PER_STEP]
    ctx_ref, m_s, l_s, acc_s, lat_s = rest[PAGES_PER_STEP:]
    step = pl.program_id(1)

    @pl.when(step == 0)
    def _():
        m_s[...] = sself_ref[0]
        l_s[...] = jnp.ones(l_s.shape, F32)
        acc_s[...] = jnp.broadcast_to(lat_new_ref[0], acc_s.shape)

    n_key_rows = N_HEADS * QK_NOPE_DIM
    n_dot = PAGES_PER_DOT * PAGE_SIZE
    lhs = jnp.concatenate([w_ukt_ref[...], qa_ref[0]], axis=0)
    qr = qr_ref[0]
    head_row = lax.broadcasted_iota(jnp.int32, (N_HEADS, n_dot), 0)
    scores = []
    for c in range(PAGES_PER_STEP // PAGES_PER_DOT):
        rows_t = jnp.concatenate([pages[c * PAGES_PER_DOT + p][...] for p in range(PAGES_PER_DOT)], axis=1)
        lat_t = rows_t[:KV_LORA_RANK].astype(BF16)
        kr_t = rows_t[KV_LORA_RANK:].astype(BF16)
        lat_s[:, c * n_dot:(c + 1) * n_dot] = lat_t
        yq = jnp.dot(lhs, lat_t, preferred_element_type=F32)
        ssq = jnp.zeros(head_row.shape, F32)
        for hd in range(N_HEADS):
            y = yq[hd * QK_NOPE_DIM:(hd + 1) * QK_NOPE_DIM]
            ssq = jnp.where(head_row == hd, jnp.sum(y * y, axis=0, keepdims=True), ssq)
        rinv = lax.rsqrt(ssq * (1.0 / QK_NOPE_DIM) + EPS)
        scores.append(yq[n_key_rows:n_key_rows + N_HEADS] * rinv
                      + jnp.dot(qr, kr_t, preferred_element_type=F32))
    s = jnp.concatenate(scores, axis=1)
    m_prev = m_s[...]
    m_new = jnp.maximum(m_prev, jnp.max(s, axis=-1, keepdims=True))
    alpha = jnp.exp2(m_prev - m_new)
    p = jnp.exp2(s - m_new)
    l_s[...] = alpha * l_s[...] + jnp.sum(p, axis=-1, keepdims=True)
    acc_s[...] = alpha * acc_s[...] + _bdot_nt(p, lat_s[...])
    m_s[...] = m_new

    @pl.when(step == pl.num_programs(1) - 1)
    def _():
        ctx_ref[0] = acc_s[...] / l_s[...]


def _sample_attention(cache_t, layer, page_table, qa, qr, sself, lat_new, w_ukt):
    nb, n_pages = page_table.shape
    n_steps = n_pages // PAGES_PER_STEP
    per_b = lambda a: pl.BlockSpec((1,) + a.shape[1:], lambda b, s, pt: (b,) + (0,) * (a.ndim - 1))
    full = lambda a: pl.BlockSpec(a.shape, lambda b, s, pt: (0,) * a.ndim)

    def page_spec(j):
        return pl.BlockSpec((None, None, CACHE_ROW, PAGE_SIZE),
                            lambda b, s, pt: (layer, pt[b * n_pages + s * PAGES_PER_STEP + j], 0, 0))

    grid_spec = pltpu.PrefetchScalarGridSpec(
        num_scalar_prefetch=1,
        grid=(nb, n_steps),
        in_specs=[per_b(qa), per_b(qr), per_b(sself), per_b(lat_new), full(w_ukt)]
        + [page_spec(j) for j in range(PAGES_PER_STEP)],
        out_specs=pl.BlockSpec((1, N_HEADS, KV_LORA_RANK), lambda b, s, pt: (b, 0, 0)),
        scratch_shapes=[pltpu.VMEM((N_HEADS, 1), F32), pltpu.VMEM((N_HEADS, 1), F32),
                        pltpu.VMEM((N_HEADS, KV_LORA_RANK), F32),
                        pltpu.VMEM((KV_LORA_RANK, PAGES_PER_STEP * PAGE_SIZE), BF16)],
    )
    return pl.pallas_call(
        _sample_attn_kernel,
        grid_spec=grid_spec,
        out_shape=jax.ShapeDtypeStruct((nb, N_HEADS, KV_LORA_RANK), F32),
        compiler_params=_cparams(2),
        name="sample_attention",
    )(page_table.reshape(-1), qa, qr, sself, lat_new, w_ukt, *([cache_t] * PAGES_PER_STEP))


def _sample_out_kernel(ctx_ref, x_ref, gate_ref, w_uv_ref, w_out_ref, o_ref):
    o_ref[...] = _attn_out(lambda hd: ctx_ref[:, hd * KV_LORA_RANK:(hd + 1) * KV_LORA_RANK],
                           x_ref[...], gate_ref[...], w_uv_ref, w_out_ref)


def _sample_out(ctx, x, gate, w_uv, w_out):
    return pl.pallas_call(
        _sample_out_kernel,
        out_shape=jax.ShapeDtypeStruct(x.shape, F32),
        compiler_params=pltpu.CompilerParams(vmem_limit_bytes=VMEM_LIMIT),
        name="sample_attention_out",
    )(ctx, x, gate, w_uv, w_out)


def _pool_mix(window_sum, h, cnt, w_pool_ref, pscale):
    parts = []
    for gi, w in enumerate(POOL_WINDOWS):
        ch = slice(gi * POOL_GROUP_DIM, (gi + 1) * POOL_GROUP_DIM)
        acc = h[:, ch]
        for kk in range(1, w):
            acc = acc + window_sum(kk, ch)
        d = acc / cnt[gi] - h[:, ch]
        parts.append(_bdot(d, w_pool_ref[gi]))
    return jnp.concatenate(parts, axis=-1) * pscale


def _pool_prompt_kernel(x_ref, shift_ref, scale_ref, gate_ref, gmix_ref, w_pool_ref, pscale_ref,
                        o_ref, st_ref, ext_s):
    si = pl.program_id(1)
    t = x_ref.shape[1]

    @pl.when(si == 0)
    def _():
        ext_s[0:POOL_HALO] = jnp.zeros((POOL_HALO, D_MODEL), F32)

    @pl.when(si > 0)
    def _():
        ext_s[0:POOL_HALO] = ext_s[t:t + POOL_HALO]

    x = x_ref[0]
    h = _rms(x, gmix_ref[...]) * (1.0 + scale_ref[0]) + shift_ref[0]
    ext_s[POOL_HALO:POOL_HALO + t] = h
    pos = si * t + lax.broadcasted_iota(jnp.int32, (t, 1), 0)
    cnt = [jnp.minimum(pos + 1, w).astype(F32) for w in POOL_WINDOWS]
    y = _pool_mix(lambda kk, ch: ext_s[POOL_HALO - kk:POOL_HALO - kk + t, ch], h, cnt,
                  w_pool_ref, pscale_ref[...])
    o_ref[0] = x + gate_ref[0] * y

    @pl.when(si == pl.num_programs(1) - 1)
    def _():
        st_ref[0] = ext_s[POOL_HALO + t - POOL_STATE_LEN:POOL_HALO + t]


def _pool_prompt(x, shift, scale, gate, gmix, w_pool, pscale, tile):
    nb, s, _ = x.shape
    tok = pl.BlockSpec((1, tile, D_MODEL), lambda b, i: (b, i, 0))
    mod = pl.BlockSpec((1, 1, D_MODEL), lambda b, i: (b, 0, 0))
    full = lambda a: pl.BlockSpec(a.shape, lambda b, i: (0,) * a.ndim)
    return pl.pallas_call(
        _pool_prompt_kernel,
        grid=(nb, s // tile),
        in_specs=[tok, mod, mod, mod, full(gmix), full(w_pool), full(pscale)],
        out_specs=[tok, pl.BlockSpec((1, POOL_STATE_LEN, D_MODEL), lambda b, i: (b, 0, 0))],
        out_shape=[jax.ShapeDtypeStruct(x.shape, F32),
                   jax.ShapeDtypeStruct((nb, POOL_STATE_LEN, D_MODEL), F32)],
        scratch_shapes=[pltpu.VMEM((POOL_HALO + tile, D_MODEL), F32)],
        compiler_params=_cparams(2),
        name="pool_prompt",
    )(x, shift, scale, gate, gmix, w_pool, pscale)


def _pool_sample_kernel(x_ref, shift_ref, scale_ref, gate_ref, gmix_ref, w_pool_ref, pscale_ref,
                        state_ref, o_ref, st_ref):
    x = x_ref[...]
    h = _rms(x, gmix_ref[...]) * (1.0 + scale_ref[...]) + shift_ref[...]
    cnt = [float(w) for w in POOL_WINDOWS]
    y = _pool_mix(lambda kk, ch: state_ref[POOL_STATE_LEN - kk, :, ch], h, cnt,
                  w_pool_ref, pscale_ref[...])
    o_ref[...] = x + gate_ref[...] * y
    for r in range(POOL_STATE_LEN - 1):
        st_ref[r] = state_ref[r + 1]
    st_ref[POOL_STATE_LEN - 1] = h


def _pool_sample(x, shift, scale, gate, gmix, w_pool, pscale, state_t):
    return pl.pallas_call(
        _pool_sample_kernel,
        out_shape=[jax.ShapeDtypeStruct(x.shape, F32), jax.ShapeDtypeStruct(state_t.shape, F32)],
        compiler_params=pltpu.CompilerParams(vmem_limit_bytes=VMEM_LIMIT),
        name="pool_sample",
    )(x, shift, scale, gate, gmix, w_pool, pscale, state_t)


NOT_TOP = 64.0
LANES = 128
SUBLANES = 8


def _sort_pairs(n):
    pairs, p = [], 1
    while p < n:
        k = p
        while k >= 1:
            for j in range(k % p, n - k, 2 * k):
                for i in range(min(k, n - j - k)):
                    if (i + j) // (2 * p) == (i + j + k) // (2 * p):
                        pairs.append((i + j, i + j + k))
            k //= 2
        p *= 2
    return pairs


def _bitonic_merge_pairs(n):
    pairs, k = [], n // 2
    while k >= 1:
        pairs += [(i, i + k) for i in range(n) if not i & k]
        k //= 2
    return pairs


def _compare_exchange(v, pairs):
    for i, j in pairs:
        v[i], v[j] = jnp.maximum(v[i], v[j]), jnp.minimum(v[i], v[j])
    return v


def _top16_sorted(s):
    v = [s[r * SUBLANES:(r + 1) * SUBLANES] for r in range(PEER_TOPK)]
    v = _compare_exchange(v, _sort_pairs(PEER_TOPK))
    for shift in (4, 2, 1):
        v = [jnp.maximum(v[i], pltpu.roll(v[PEER_TOPK - 1 - i], shift, axis=0)) for i in range(PEER_TOPK)]
        v = _compare_exchange(v, _bitonic_merge_pairs(PEER_TOPK))
    return v


def _kth_largest(work, k):
    for _ in range(k):
        m = jnp.max(work, axis=0, keepdims=True)
        work = jnp.where(work == m, NEG_INF, work)
    return m


def _prefix_count(x, thresholds, above):
    th = thresholds
    parts = []
    for r in range(x.shape[0] // SUBLANES):
        xr = x[r * SUBLANES:(r + 1) * SUBLANES]
        passes = (lambda t: t > xr) if above else (lambda t: xr >= t)
        c8 = passes(th[7])
        c4 = passes(jnp.where(c8, th[11], th[3]))
        c2 = passes(jnp.where(c8, jnp.where(c4, th[13], th[9]), jnp.where(c4, th[5], th[1])))
        c1 = passes(jnp.where(c8, jnp.where(c4, jnp.where(c2, th[14], th[12]), jnp.where(c2, th[10], th[8])),
                              jnp.where(c4, jnp.where(c2, th[6], th[4]), jnp.where(c2, th[2], th[0]))))
        cnt = (jnp.where(c8, 8.0, 0.0) + jnp.where(c4, 4.0, 0.0)) + (jnp.where(c2, 2.0, 0.0) + jnp.where(c1, 1.0, 0.0))
        parts.append(jnp.where(passes(th[15]), float(PEER_TOPK), cnt))
    return jnp.concatenate(parts, axis=0)


def _dup_bf16(x):
    hi = pltpu.bitcast(x.astype(BF16).astype(F32), jnp.uint32)
    return hi | (hi >> 16)


def _pack_pairs(x):
    return pltpu.bitcast(x.astype(BF16), jnp.uint32)


def _peer_route_kernel(x_ref, shift_ref, scale_ref, gffn_ref, wqt_ref, subk_ref,
                       ht_ref, r2_ref, e2_ref, n1_ref, e1_ref):
    t = x_ref.shape[1]
    h = _rms(x_ref[0], gffn_ref[...]) * (1.0 + scale_ref[0]) + shift_ref[0]
    ht = h.T.astype(BF16)
    ht_ref[...] = pltpu.bitcast(ht, jnp.uint32)
    qt = jnp.dot(wqt_ref[...], ht, preferred_element_type=F32).astype(BF16)
    row8 = lax.broadcasted_iota(jnp.int32, (SUBLANES, LANES), 0)
    for hd in range(PEER_HEADS):
        sc = [jnp.dot(subk_ref[2 * hd + p], qt[(2 * hd + p) * PEER_HALF:(2 * hd + p + 1) * PEER_HALF],
                      preferred_element_type=F32) for p in range(2)]
        for lt in range(t // LANES):
            ln = slice(lt * LANES, (lt + 1) * LANES)
            s1, s2 = sc[0][:, ln], sc[1][:, ln]
            top1, top2 = _top16_sorted(s1), _top16_sorted(s2)
            v1_lo, v1_hi = top1[0], top1[SUBLANES]
            for a in range(1, SUBLANES):
                v1_lo = jnp.where(row8 == a, top1[a], v1_lo)
                v1_hi = jnp.where(row8 == a, top1[SUBLANES + a], v1_hi)
            t1, t2 = top1, top2
            below = lambda r: row8 < r
            cand = jnp.concatenate([
                v1_lo + t2[0],
                v1_hi + t2[0],
                v1_lo + t2[1],
                jnp.where(below(5), v1_lo, jnp.where(row8 == 5, t1[0], jnp.where(row8 == 6, t1[1], t1[2])))
                + jnp.where(below(5), t2[2], t2[4]),
                jnp.where(below(4), v1_lo, jnp.where((row8 & 1) == 0, t1[0], t1[1]))
                + jnp.where(below(4), t2[3], jnp.where(below(6), t2[5], t2[6])),
                jnp.where(below(2), v1_lo, t1[0])
                + jnp.where(below(2), t2[7], jnp.where(below(3), t2[8], jnp.where(below(4), t2[9], jnp.where(
                    below(5), t2[10], jnp.where(below(6), t2[11], jnp.where(below(7), t2[12], t2[13])))))),
                t1[0] + jnp.where(below(1), t2[14], jnp.where(below(2), t2[15], NEG_INF)),
            ], axis=0)
            tau = _kth_largest(cand, PEER_TOPK)
            cmax = top1[0] + top2[0]
            z = jnp.sum(jnp.where(cand >= tau, jnp.exp(cand - cmax[:1]), 0.0), axis=0, keepdims=True)
            theta = []
            for b in range(PEER_TOPK):
                if b >= SUBLANES:
                    theta.append(jnp.where(t1[0] + t2[b] >= tau, t1[0], -NEG_INF)[:1])
                    continue
                lo = jnp.where(v1_lo + t2[b] >= tau, v1_lo, -NEG_INF)
                if b == 0:
                    lo = jnp.minimum(lo, jnp.where(v1_hi + t2[b] >= tau, v1_hi, -NEG_INF))
                theta.append(jnp.min(lo, axis=0, keepdims=True))
            n1 = _prefix_count(s1, theta, above=False)
            rank2 = _prefix_count(s2, top2, above=True)
            rank2 = jnp.where(rank2 >= PEER_TOPK, NOT_TOP, rank2)
            r2_ref[hd, :, ln] = _pack_pairs(rank2)
            e2_ref[hd, :, ln] = _pack_pairs(jnp.exp(s2 - top2[0][:1]))
            n1_ref[hd, lt] = _dup_bf16(n1)
            e1_ref[hd, lt] = _dup_bf16(jnp.exp(s1 - top1[0][:1]) * (0.5 / z))


def _peer_route(x, shift, scale, gffn, wqt, subk, tile):
    nb, s, _ = x.shape
    r = shift.shape[1]
    ns = s // tile
    n = nb * s
    rt = tile if r == s else 1
    tok = pl.BlockSpec((1, tile, D_MODEL), lambda b, i: (b, i, 0))
    mod = pl.BlockSpec((1, rt, D_MODEL), (lambda b, i: (b, i, 0)) if r == s else (lambda b, i: (b, 0, 0)))
    full = lambda a: pl.BlockSpec(a.shape, lambda b, i: (0,) * a.ndim)
    u32 = jnp.uint32
    lane_tiles = tile // LANES
    pairs = pl.BlockSpec((PEER_HEADS, PEER_N_KEYS // 2, tile), lambda b, i: (0, 0, b * ns + i))
    rows = pl.BlockSpec((PEER_HEADS, lane_tiles, PEER_N_KEYS, LANES), lambda b, i: (0, b * ns + i, 0, 0))
    pairs_shape = jax.ShapeDtypeStruct((PEER_HEADS, PEER_N_KEYS // 2, n), u32)
    rows_shape = jax.ShapeDtypeStruct((PEER_HEADS, n // LANES, PEER_N_KEYS, LANES), u32)
    return pl.pallas_call(
        _peer_route_kernel,
        grid=(nb, ns),
        in_specs=[tok, mod, mod, full(gffn), full(wqt), full(subk)],
        out_specs=[pl.BlockSpec((D_MODEL // 2, tile), lambda b, i: (0, b * ns + i)),
                   pairs, pairs, rows, rows],
        out_shape=[jax.ShapeDtypeStruct((D_MODEL // 2, n), u32), pairs_shape, pairs_shape,
                   rows_shape, rows_shape],
        compiler_params=_cparams(2),
        name="peer_route",
    )(x, shift, scale, gffn, wqt, subk)


N_CHUNKS = PEER_N_EXPERTS // EXPERT_CHUNK
GATE_ROWS = 32
OUT_DEPTH = 1024
OUT_ROWS = 1024
KEYS_PER_PIECE = 8


def _peer_sweep_kernel(ht_ref, r2_ref, e2_ref, n1_ref, e1_ref, u_ref, vt_ref, x_ref, gate_ref, o_ref,
                       ga_s, yt_s):
    j = pl.program_id(2)
    t = ht_ref.shape[1]

    @pl.when(j == 0)
    def _():
        yt_s[...] = jnp.zeros(yt_s.shape, F32)

    ht = pltpu.bitcast(ht_ref[...], BF16)
    zero = jnp.zeros((GATE_ROWS, LANES), BF16)
    n_groups = PEER_N_KEYS // GATE_ROWS

    def gate_piece(k):
        n_rows = KEYS_PER_PIECE * PEER_N_KEYS
        u = pltpu.bitcast(u_ref[k * n_rows // 2:(k + 1) * n_rows // 2, :], BF16)
        act = jnp.dot(u, ht, preferred_element_type=F32)
        ga_s[k * n_rows:(k + 1) * n_rows, :] = (act * (1.0 + lax.erf(act * (2.0 ** -0.5)))).astype(BF16)
        for lt in range(t // LANES):
            ln = slice(lt * LANES, (lt + 1) * LANES)
            g = [[zero] * n_groups for _ in range(KEYS_PER_PIECE)]
            for hd in range(PEER_HEADS):
                n1, e1 = [], []
                for kk in range(KEYS_PER_PIECE):
                    row = pl.ds(j * KEY_ROWS_PER_CHUNK + k * KEYS_PER_PIECE + kk, 1)
                    spread = lambda ref: pltpu.bitcast(
                        jnp.broadcast_to(ref[hd, lt, row, :], (GATE_ROWS // 2, LANES)), BF16)
                    n1.append(spread(n1_ref))
                    e1.append(spread(e1_ref))
                for sg in range(n_groups):
                    pairs = slice(sg * GATE_ROWS // 2, (sg + 1) * GATE_ROWS // 2)
                    r2 = pltpu.bitcast(r2_ref[hd, pairs, ln], BF16)
                    e2 = pltpu.bitcast(e2_ref[hd, pairs, ln], BF16)
                    for kk in range(KEYS_PER_PIECE):
                        g[kk][sg] = g[kk][sg] + jnp.where(r2 < n1[kk], e2 * e1[kk], zero)
            for kk in range(KEYS_PER_PIECE):
                for sg in range(n_groups):
                    r0 = k * n_rows + kk * PEER_N_KEYS + sg * GATE_ROWS
                    ga_s[r0:r0 + GATE_ROWS, ln] = ga_s[r0:r0 + GATE_ROWS, ln] * g[kk][sg]

    def out_piece(group, m):
        cols = slice(group * OUT_DEPTH, (group + 1) * OUT_DEPTH)
        rows = slice(m * OUT_ROWS, (m + 1) * OUT_ROWS)
        v = pltpu.bitcast(vt_ref[m * OUT_ROWS // 2:(m + 1) * OUT_ROWS // 2, cols], BF16)
        yt_s[rows, :] += jnp.dot(v, ga_s[cols, :], preferred_element_type=F32)

    pieces_per_group = OUT_DEPTH // (KEYS_PER_PIECE * PEER_N_KEYS)
    n_pieces = KEY_ROWS_PER_CHUNK // KEYS_PER_PIECE
    pending = []
    for k in range(n_pieces):
        gate_piece(k)
        if pending:
            out_piece(*pending.pop(0))
        if (k + 1) % pieces_per_group == 0:
            pending += [(k // pieces_per_group, m) for m in range(D_MODEL // OUT_ROWS)]
    for piece in pending:
        out_piece(*piece)

    @pl.when(j == pl.num_programs(2) - 1)
    def _():
        o_ref[0] = x_ref[0] + gate_ref[0] * yt_s[...].T


def _peer_sweep(ht, r2, e2, n1, e1, u, vt, x, gate, tile):
    nb, s, _ = x.shape
    r = gate.shape[1]
    ns = s // tile
    pairs = pl.BlockSpec((PEER_HEADS, PEER_N_KEYS // 2, tile), lambda b, i, j: (0, 0, b * ns + i))
    rows = pl.BlockSpec((PEER_HEADS, tile // LANES, PEER_N_KEYS, LANES),
                        lambda b, i, j: (0, b * ns + i, 0, 0))
    tok = pl.BlockSpec((1, tile, D_MODEL), lambda b, i, j: (b, i, 0))
    mod = pl.BlockSpec((1, tile if r == s else 1, D_MODEL),
                       (lambda b, i, j: (b, i, 0)) if r == s else (lambda b, i, j: (b, 0, 0)))
    return pl.pallas_call(
        _peer_sweep_kernel,
        grid=(nb, ns, N_CHUNKS),
        in_specs=[pl.BlockSpec((D_MODEL // 2, tile), lambda b, i, j: (0, b * ns + i)),
                  pairs, pairs, rows, rows,
                  pl.BlockSpec((EXPERT_CHUNK // 2, D_MODEL), lambda b, i, j: (j, 0)),
                  pl.BlockSpec((D_MODEL // 2, EXPERT_CHUNK), lambda b, i, j: (0, j)),
                  tok, mod],
        out_specs=tok,
        out_shape=jax.ShapeDtypeStruct(x.shape, F32),
        scratch_shapes=[pltpu.VMEM((EXPERT_CHUNK, tile), BF16), pltpu.VMEM((D_MODEL, tile), F32)],
        compiler_params=_cparams(3),
        name="peer_sweep",
    )(ht, r2, e2, n1, e1, u, vt, x, gate)


def _peer(x, shift, scale, gate, gffn, pw, route_tile, sweep_tile):
    routed = _peer_route(x, shift, scale, gffn, pw["wqt"], pw["subk"], route_tile)
    return _peer_sweep(*routed, pw["u"], pw["vt"], x, gate, sweep_tile)


def _rope_tables(pos):
    inv = ROPE_THETA ** (-jnp.arange(0, QK_ROPE_DIM, 2, dtype=F32) / QK_ROPE_DIM)
    ang = pos.astype(F32)[:, None] * inv[None, :]
    c, s, z = jnp.cos(ang), jnp.sin(ang), jnp.zeros_like(ang)
    return jnp.concatenate([c, z, c, z], -1), jnp.concatenate([-s, z, s, z], -1)


def _pad_rope_cols(w):
    z = jnp.zeros(w.shape[:-1] + (ROPE_HALF,), w.dtype)
    return jnp.concatenate([w[..., :ROPE_HALF], z, w[..., ROPE_HALF:], z], -1)


def _mla_weights(w_in, g_q_lat, g_kv_lat, w_uq, w_uk, w_uv, g_qn_nope, g_qn_rope, g_kn_nope,
                 g_kn_rope, w_out):
    lat_end = Q_LORA_RANK + KV_LORA_RANK
    w_in_p = jnp.concatenate([w_in[:, :lat_end], _pad_rope_cols(w_in[:, lat_end:])], -1)
    wq = w_uq.reshape(Q_LORA_RANK, N_HEADS, QK_NOPE_DIM + QK_ROPE_DIM)
    wq = jnp.concatenate([wq[..., :QK_NOPE_DIM], _pad_rope_cols(wq[..., QK_NOPE_DIM:])], -1)
    row = lambda g: g.reshape(1, -1)
    return {
        "w_in": w_in_p.astype(BF16),
        "g_q_lat": row(g_q_lat), "g_kv_lat": row(g_kv_lat), "g_kr": row(_pad_rope_cols(g_kn_rope)),
        "w_uq": wq.reshape(Q_LORA_RANK, N_HEADS * HEAD_PAD).astype(BF16),
        "g_qn": row(g_qn_nope), "g_qr": row(_pad_rope_cols(g_qn_rope)),
        "w_uk": w_uk.reshape(KV_LORA_RANK, N_HEADS * QK_NOPE_DIM).astype(BF16),
        "g_kn": row(g_kn_nope),
        "w_ukt": jnp.transpose(w_uk, (1, 2, 0)).astype(BF16),
        "w_uv": jnp.transpose(w_uv, (1, 0, 2)).astype(BF16),
        "w_out": w_out.astype(BF16),
    }


def _pack_rows_kernel(x_ref, o_ref):
    o_ref[...] = _pack_pairs(x_ref[...])


def _pack_cols_kernel(x_ref, o_ref):
    o_ref[...] = _pack_pairs(x_ref[...].T)


def _pack_weight(w, layer, transpose):
    _, r, c = w.shape
    blk = 1024
    src = pl.BlockSpec((None, blk, c), lambda i: (layer, i, 0))
    if transpose:
        return pl.pallas_call(
            _pack_cols_kernel, grid=(r // blk,), in_specs=[src],
            out_specs=pl.BlockSpec((c // 2, blk), lambda i: (0, i)),
            out_shape=jax.ShapeDtypeStruct((c // 2, r), jnp.uint32),
            compiler_params=_cparams(1), name="pack_weight_t")(w)
    return pl.pallas_call(
        _pack_rows_kernel, grid=(r // blk,), in_specs=[src],
        out_specs=pl.BlockSpec((blk // 2, c), lambda i: (i, 0)),
        out_shape=jax.ShapeDtypeStruct((r // 2, c), jnp.uint32),
        compiler_params=_cparams(1), name="pack_weight")(w)


def _peer_weights(w_q, subkeys, u, v, layer):
    return {
        "wqt": w_q[layer].T.astype(BF16),
        "subk": subkeys[layer].reshape(2 * PEER_HEADS, PEER_N_KEYS, PEER_HALF).astype(BF16),
        "u": _pack_weight(u, layer, False),
        "vt": _pack_weight(v, layer, True),
    }


PROMPT_TILE = 256
SAMPLE_TILE = 128
PEER_SWEEP_TILE = 512
ATTN_TILE = 512
Q_SUB = 256


def kernel(x_prompt, x_sample, c_prompt, c_sample, cache_mla, state_pool, page_table, norm_mix, norm_ffn, w_ada, b_ada, w_mla_in, g_q_lat, g_kv_lat, w_uq, w_uk, w_uv, g_qn_nope, g_qn_rope, g_kn_nope, g_kn_rope, w_mla_out, w_pool, pool_scale, w_peer_q, peer_subkeys, peer_u, peer_v):
    nbp, seq, _ = x_prompt.shape
    nbs = x_sample.shape[0]
    depth = w_ada.shape[0]
    xp = x_prompt
    xs = x_sample.reshape(1, nbs, D_MODEL)

    mods = _ada_params(jnp.concatenate([c_prompt, c_sample], 0), w_ada, b_ada)
    mods = mods.reshape(depth, nbp + nbs, 6, D_MODEL)
    cs_p, sn_p = _rope_tables(jnp.arange(seq))
    cs_s, sn_s = _rope_tables(jnp.full((nbs,), PAST_LEN))
    cache_t = jnp.swapaxes(cache_mla, 2, 3)

    rows_p, rows_s, pst_p, pst_s = [], [], [], []
    for l in range(depth):
        mp = [mods[l, :nbp, k].reshape(nbp, 1, D_MODEL) for k in range(6)]
        ms = [mods[l, nbp:, k].reshape(1, nbs, D_MODEL) for k in range(6)]
        gmix = norm_mix[l].reshape(1, D_MODEL)
        gffn = norm_ffn[l].reshape(1, D_MODEL)
        i = l // 2
        if l % 2 == 0:
            mw = _mla_weights(w_mla_in[i], g_q_lat[i], g_kv_lat[i], w_uq[i], w_uk[i], w_uv[i],
                              g_qn_nope[i], g_qn_rope[i], g_kn_nope[i], g_kn_rope[i], w_mla_out[i])
            r_p, q, k, v = _mla_project(xp, mp[0], mp[1], gmix, cs_p, sn_p, mw, PROMPT_TILE, False)
            xp = _prompt_attention(q, k, v, xp, mp[2], mw["w_uv"], mw["w_out"], ATTN_TILE)
            r_s, q, k, v, qa, sself = _mla_project(xs, ms[0], ms[1], gmix, cs_s, sn_s, mw,
                                                   SAMPLE_TILE, True)
            q = q.reshape(nbs, N_HEADS, HEAD_PAD)[:, :, QK_NOPE_DIM:]
            qr = jnp.concatenate([q[..., :ROPE_HALF], q[..., 2 * ROPE_HALF:3 * ROPE_HALF]], -1)
            qa = jnp.pad(qa.reshape(nbs, N_HEADS, KV_LORA_RANK), ((0, 0), (0, QA_ROWS - N_HEADS), (0, 0)))
            ctx = _sample_attention(
                cache_t, i, page_table, qa, qr, sself.reshape(nbs, 128)[:, :N_HEADS, None],
                r_s.reshape(nbs, 1, CACHE_ROW)[..., :KV_LORA_RANK],
                mw["w_ukt"].reshape(N_HEADS * QK_NOPE_DIM, KV_LORA_RANK))
            xs = _sample_out(ctx.reshape(nbs, N_HEADS * KV_LORA_RANK), xs[0], ms[2][0],
                             mw["w_uv"], mw["w_out"]).reshape(1, nbs, D_MODEL)
            rows_p.append(r_p)
            rows_s.append(r_s.reshape(nbs, 1, CACHE_ROW))
        else:
            wp = w_pool[i].astype(BF16)
            psc = pool_scale[i].reshape(1, D_MODEL)
            xp, st_p = _pool_prompt(xp, mp[0], mp[1], mp[2], gmix, wp, psc, PROMPT_TILE)
            x2, st_s = _pool_sample(xs[0], ms[0][0], ms[1][0], ms[2][0], gmix, wp, psc,
                                    jnp.transpose(state_pool[i], (1, 0, 2)))
            xs = x2.reshape(1, nbs, D_MODEL)
            pst_p.append(st_p)
            pst_s.append(jnp.transpose(st_s, (1, 0, 2)))
        pw = _peer_weights(w_peer_q, peer_subkeys, peer_u, peer_v, l)
        xp = _peer(xp, mp[3], mp[4], mp[5], gffn, pw, PROMPT_TILE, PEER_SWEEP_TILE)
        xs = _peer(xs, ms[3], ms[4], ms[5], gffn, pw, SAMPLE_TILE, SAMPLE_TILE)
    return (xp, xs.reshape(nbs, 1, D_MODEL), jnp.stack(rows_p), jnp.stack(rows_s),
            jnp.stack(pst_p), jnp.stack(pst_s))
```

```python
import functools

import jax
import jax.numpy as jnp
from jax import lax
from jax.experimental import pallas as pl
from jax.experimental.pallas import tpu as pltpu

F32 = jnp.float32
BF16 = jnp.bfloat16

D_MODEL = 1024
EPS = 1e-6
PAST_LEN = 8192
PAGE_SIZE = 128
N_HEADS = 8
QK_NOPE_DIM = 128
QK_ROPE_DIM = 64
ROPE_HALF = QK_ROPE_DIM // 2
ROPE_PAD = 128
HEAD_PAD = QK_NOPE_DIM + ROPE_PAD
V_HEAD_DIM = 128
Q_LORA_RANK = 384
KV_LORA_RANK = 256
CACHE_ROW = KV_LORA_RANK + QK_ROPE_DIM
ROPE_THETA = 10000.0
ATTN_SCALE = (QK_NOPE_DIM + QK_ROPE_DIM) ** -0.5
LOG2_E = 1.4426950408889634
POOL_WINDOWS = (2, 4, 8, 16)
POOL_GROUP_DIM = D_MODEL // len(POOL_WINDOWS)
POOL_STATE_LEN = max(POOL_WINDOWS) - 1
POOL_HALO = 16
PEER_HEADS = 8
PEER_N_KEYS = 128
PEER_N_EXPERTS = PEER_N_KEYS * PEER_N_KEYS
PEER_TOPK = 16
PEER_HALF = 128
EXPERT_CHUNK = 2048
KEY_ROWS_PER_CHUNK = EXPERT_CHUNK // PEER_N_KEYS

VMEM_LIMIT = 56 * 1024 * 1024

NEG_INF = float("-inf")


def _cparams(n_grid):
    return pltpu.CompilerParams(
        dimension_semantics=("arbitrary",) * n_grid, vmem_limit_bytes=VMEM_LIMIT)


def _rms(x, g, n=None):
    n = x.shape[-1] if n is None else n
    ms = jnp.sum(x * x, axis=-1, keepdims=True) * (1.0 / n)
    return x * lax.rsqrt(ms + EPS) * g


def _bdot(a, b):
    return jnp.dot(a.astype(BF16), b.astype(BF16), preferred_element_type=F32)


def _bdot_nt(a, b):
    return lax.dot_general(a.astype(BF16), b.astype(BF16), (((1,), (1,)), ((), ())),
                           preferred_element_type=F32)


def _mods_kernel(c_ref, w_ref, b_ref, o_ref):
    o_ref[0] = _bdot(jax.nn.silu(c_ref[...]), w_ref[0]) + b_ref[0]


def _ada_params(c_all, w_ada, b_ada):
    n_layers = w_ada.shape[0]
    n_rows = c_all.shape[0]
    n_col = w_ada.shape[2] // D_MODEL
    return pl.pallas_call(
        _mods_kernel,
        grid=(n_layers, n_col),
        in_specs=[
            pl.BlockSpec((n_rows, D_MODEL), lambda l, k: (0, 0)),
            pl.BlockSpec((1, D_MODEL, D_MODEL), lambda l, k: (l, 0, k)),
            pl.BlockSpec((1, 1, D_MODEL), lambda l, k: (l, 0, k)),
        ],
        out_specs=pl.BlockSpec((1, n_rows, D_MODEL), lambda l, k: (l, 0, k)),
        out_shape=jax.ShapeDtypeStruct((n_layers, n_rows, n_col * D_MODEL), F32),
        compiler_params=_cparams(2),
        name="ada_params",
    )(c_all, w_ada, b_ada.reshape(n_layers, 1, -1))


def _rope_pad(x, g, cs, sn):
    y = _rms(x, g, QK_ROPE_DIM)
    return y * cs + pltpu.roll(y, ROPE_PAD // 2, axis=1) * sn


def _mla_proj_kernel(absorb, x_ref, shift_ref, scale_ref, gmix_ref, cs_ref, sn_ref, w_in_ref,
                     g_q_lat_ref, g_kv_lat_ref, g_kr_ref, w_uq_ref, g_qn_ref, g_qr_ref,
                     w_uk_ref, g_kn_ref, *rest):
    if absorb:
        w_ukt_ref, rows_ref, q_ref, k_ref, v_ref, qa_ref, sself_ref = rest
    else:
        rows_ref, q_ref, k_ref, v_ref = rest
    cs = cs_ref[...]
    sn = sn_ref[...]
    h = _rms(x_ref[0], gmix_ref[...]) * (1.0 + scale_ref[0]) + shift_ref[0]
    z = _bdot(h, w_in_ref[...])
    cq = _rms(z[:, :Q_LORA_RANK], g_q_lat_ref[...])
    lat = _rms(z[:, Q_LORA_RANK:Q_LORA_RANK + KV_LORA_RANK], g_kv_lat_ref[...])
    kr = _rope_pad(z[:, Q_LORA_RANK + KV_LORA_RANK:], g_kr_ref[...], cs, sn)
    rows_ref[0, :, :KV_LORA_RANK] = lat
    rows_ref[0, :, KV_LORA_RANK:] = (kr + pltpu.roll(kr, ROPE_PAD - ROPE_HALF, axis=1))[:, :QK_ROPE_DIM]
    v_ref[0] = lat.astype(BF16)
    q = _bdot(cq, w_uq_ref[...])
    kn_all = _bdot(lat, w_uk_ref[...])
    kr_b = kr.astype(BF16)
    if absorb:
        lane = lax.broadcasted_iota(jnp.int32, (x_ref.shape[1], 128), 1)
        sself = jnp.zeros((x_ref.shape[1], 128), F32)
    for hd in range(N_HEADS):
        o = hd * HEAD_PAD
        qn = _rms(q[:, o:o + QK_NOPE_DIM], g_qn_ref[...]) * (ATTN_SCALE * LOG2_E)
        qr = _rope_pad(q[:, o + QK_NOPE_DIM:o + HEAD_PAD], g_qr_ref[...], cs, sn) * (ATTN_SCALE * LOG2_E)
        kn = _rms(kn_all[:, hd * QK_NOPE_DIM:(hd + 1) * QK_NOPE_DIM], g_kn_ref[...])
        qn_b, qr_b, kn_b = qn.astype(BF16), qr.astype(BF16), kn.astype(BF16)
        q_ref[0, :, o:o + QK_NOPE_DIM] = qn_b
        q_ref[0, :, o + QK_NOPE_DIM:o + HEAD_PAD] = qr_b
        k_ref[0, :, o:o + QK_NOPE_DIM] = kn_b
        k_ref[0, :, o + QK_NOPE_DIM:o + HEAD_PAD] = kr_b
        if absorb:
            s = (jnp.sum(qn_b.astype(F32) * kn_b.astype(F32), axis=-1, keepdims=True)
                 + jnp.sum(qr_b.astype(F32) * kr_b.astype(F32), axis=-1, keepdims=True))
            sself = jnp.where(lane == hd, s, sself)
            qa = _bdot(qn * g_kn_ref[...], w_ukt_ref[hd])
            qa_ref[0, :, hd * KV_LORA_RANK:(hd + 1) * KV_LORA_RANK] = qa.astype(BF16)
    if absorb:
        sself_ref[0] = sself


def _mla_project(x, shift, scale, gmix, cs, sn, mw, tile, absorb):
    nb, s, _ = x.shape
    r = shift.shape[1]
    ns = s // tile
    rt = tile if r == s else 1
    tok = lambda bs: pl.BlockSpec(bs, lambda b, i: (b, i, 0))
    mod = pl.BlockSpec((1, rt, D_MODEL), (lambda b, i: (b, i, 0)) if r == s else (lambda b, i: (b, 0, 0)))
    full = lambda a: pl.BlockSpec(a.shape, lambda b, i: (0,) * a.ndim)
    weights = [mw["w_in"], mw["g_q_lat"], mw["g_kv_lat"], mw["g_kr"], mw["w_uq"], mw["g_qn"],
               mw["g_qr"], mw["w_uk"], mw["g_kn"]]
    if absorb:
        weights.append(mw["w_ukt"])
    out_shape = [jax.ShapeDtypeStruct((nb, s, CACHE_ROW), F32),
                 jax.ShapeDtypeStruct((nb, s, N_HEADS * HEAD_PAD), BF16),
                 jax.ShapeDtypeStruct((nb, s, N_HEADS * HEAD_PAD), BF16),
                 jax.ShapeDtypeStruct((nb, s, KV_LORA_RANK), BF16)]
    out_specs = [tok((1, tile, CACHE_ROW)), tok((1, tile, N_HEADS * HEAD_PAD)),
                 tok((1, tile, N_HEADS * HEAD_PAD)), tok((1, tile, KV_LORA_RANK))]
    if absorb:
        out_shape += [jax.ShapeDtypeStruct((nb, s, N_HEADS * KV_LORA_RANK), BF16),
                      jax.ShapeDtypeStruct((nb, s, 128), F32)]
        out_specs += [tok((1, tile, N_HEADS * KV_LORA_RANK)), tok((1, tile, 128))]
    return pl.pallas_call(
        functools.partial(_mla_proj_kernel, absorb),
        grid=(nb, ns),
        in_specs=[tok((1, tile, D_MODEL)), mod, mod, full(gmix),
                  pl.BlockSpec((tile, ROPE_PAD), lambda b, i: (i, 0)),
                  pl.BlockSpec((tile, ROPE_PAD), lambda b, i: (i, 0))] + [full(w) for w in weights],
        out_specs=out_specs,
        out_shape=out_shape,
        compiler_params=_cparams(2),
        name="mla_project_sample" if absorb else "mla_project",
    )(x, shift, scale, gmix, cs, sn, *weights)


def _attn_out(ctx_of_head, x, gate, w_uv_ref, w_out_ref):
    o = jnp.concatenate([_bdot(ctx_of_head(hd), w_uv_ref[hd]) for hd in range(N_HEADS)], axis=-1)
    return x + gate * _bdot(o, w_out_ref[...])


def _flash_kernel(q_ref, k_ref, v_ref, x_ref, gate_ref, w_uv_ref, w_out_ref, o_ref,
                  m_s, l_s, acc_s):
    qi = pl.program_id(1)
    ki = pl.program_id(2)
    tq = q_ref.shape[1]

    @pl.when(ki == 0)
    def _():
        m_s[...] = jnp.full(m_s.shape, NEG_INF, F32)
        l_s[...] = jnp.zeros(l_s.shape, F32)
        acc_s[...] = jnp.zeros(acc_s.shape, F32)

    def step(masked):
        v = v_ref[0]
        q_sub = min(Q_SUB, tq)
        for qs in range(tq // q_sub):
            rows = slice(qs * q_sub, (qs + 1) * q_sub)
            if masked:
                row = lax.broadcasted_iota(jnp.int32, (q_sub, tq), 0) + qs * q_sub
                col = lax.broadcasted_iota(jnp.int32, (q_sub, tq), 1)
                keep = col <= row
            for hd in range(N_HEADS):
                o = hd * HEAD_PAD
                s = _bdot_nt(q_ref[0, rows, o:o + HEAD_PAD], k_ref[0, :, o:o + HEAD_PAD])
                if masked:
                    s = jnp.where(keep, s, NEG_INF)
                m_prev = m_s[hd, rows]
                m_new = jnp.maximum(m_prev, jnp.max(s, axis=-1, keepdims=True))
                alpha = jnp.exp2(m_prev - m_new)
                p = jnp.exp2(s - m_new)
                l_s[hd, rows] = alpha * l_s[hd, rows] + jnp.sum(p, axis=-1, keepdims=True)
                acc_s[hd, rows] = alpha * acc_s[hd, rows] + _bdot(p, v)
                m_s[hd, rows] = m_new

    @pl.when(ki < qi)
    def _():
        step(False)

    @pl.when(ki == qi)
    def _():
        step(True)
        o_ref[0] = _attn_out(lambda hd: acc_s[hd] / l_s[hd], x_ref[0], gate_ref[0],
                             w_uv_ref, w_out_ref)


def _prompt_attention(q, k, v, x, gate, w_uv, w_out, tile):
    nb, s, _ = x.shape
    nt = s // tile
    full = lambda a: pl.BlockSpec(a.shape, lambda b, i, j: (0,) * a.ndim)
    return pl.pallas_call(
        _flash_kernel,
        grid=(nb, nt, nt),
        in_specs=[
            pl.BlockSpec((1, tile, N_HEADS * HEAD_PAD), lambda b, i, j: (b, i, 0)),
            pl.BlockSpec((1, tile, N_HEADS * HEAD_PAD), lambda b, i, j: (b, jnp.minimum(i, j), 0)),
            pl.BlockSpec((1, tile, KV_LORA_RANK), lambda b, i, j: (b, jnp.minimum(i, j), 0)),
            pl.BlockSpec((1, tile, D_MODEL), lambda b, i, j: (b, i, 0)),
            pl.BlockSpec((1, 1, D_MODEL), lambda b, i, j: (b, 0, 0)),
            full(w_uv), full(w_out),
        ],
        out_specs=pl.BlockSpec((1, tile, D_MODEL), lambda b, i, j: (b, i, 0)),
        out_shape=jax.ShapeDtypeStruct(x.shape, F32),
        scratch_shapes=[pltpu.VMEM((N_HEADS, tile, 1), F32), pltpu.VMEM((N_HEADS, tile, 1), F32),
                        pltpu.VMEM((N_HEADS, tile, KV_LORA_RANK), F32)],
        compiler_params=_cparams(3),
        name="prompt_attention",
    )(q, k, v, x, gate, w_uv, w_out)


PAGES_PER_STEP = 64
PAGES_PER_DOT = 8
QA_ROWS = 16


def _sample_attn_kernel(pt_ref, qa_ref, qr_ref, sself_ref, lat_new_ref, w_ukt_ref, *rest):
    del pt_ref
    pages = rest[:PAGES_PER_STEP]
    ctx_ref, m_s, l_s, acc_s, lat_s = rest[PAGES_PER_STEP:]
    step = pl.program_id(1)

    @pl.when(step == 0)
    def _():
        m_s[...] = sself_ref[0]
        l_s[...] = jnp.ones(l_s.shape, F32)
        acc_s[...] = jnp.broadcast_to(lat_new_ref[0], acc_s.shape)

    n_key_rows = N_HEADS * QK_NOPE_DIM
    n_dot = PAGES_PER_DOT * PAGE_SIZE
    lhs = jnp.concatenate([w_ukt_ref[...], qa_ref[0]], axis=0)
    qr = qr_ref[0]
    head_row = lax.broadcasted_iota(jnp.int32, (N_HEADS, n_dot), 0)
    scores = []
    for c in range(PAGES_PER_STEP // PAGES_PER_DOT):
        rows_t = jnp.concatenate([pages[c * PAGES_PER_DOT + p][...] for p in range(PAGES_PER_DOT)], axis=1)
        lat_t = rows_t[:KV_LORA_RANK].astype(BF16)
        kr_t = rows_t[KV_LORA_RANK:].astype(BF16)
        lat_s[:, c * n_dot:(c + 1) * n_dot] = lat_t
        yq = jnp.dot(lhs, lat_t, preferred_element_type=F32)
        ssq = jnp.zeros(head_row.shape, F32)
        for hd in range(N_HEADS):
            y = yq[hd * QK_NOPE_DIM:(hd + 1) * QK_NOPE_DIM]
            ssq = jnp.where(head_row == hd, jnp.sum(y * y, axis=0, keepdims=True), ssq)
        rinv = lax.rsqrt(ssq * (1.0 / QK_NOPE_DIM) + EPS)
        scores.append(yq[n_key_rows:n_key_rows + N_HEADS] * rinv
                      + jnp.dot(qr, kr_t, preferred_element_type=F32))
    s = jnp.concatenate(scores, axis=1)
    m_prev = m_s[...]
    m_new = jnp.maximum(m_prev, jnp.max(s, axis=-1, keepdims=True))
    alpha = jnp.exp2(m_prev - m_new)
    p = jnp.exp2(s - m_new)
    l_s[...] = alpha * l_s[...] + jnp.sum(p, axis=-1, keepdims=True)
    acc_s[...] = alpha * acc_s[...] + _bdot_nt(p, lat_s[...])
    m_s[...] = m_new

    @pl.when(step == pl.num_programs(1) - 1)
    def _():
        ctx_ref[0] = acc_s[...] / l_s[...]


def _sample_attention(cache_t, layer, page_table, qa, qr, sself, lat_new, w_ukt):
    nb, n_pages = page_table.shape
    n_steps = n_pages // PAGES_PER_STEP
    per_b = lambda a: pl.BlockSpec((1,) + a.shape[1:], lambda b, s, pt: (b,) + (0,) * (a.ndim - 1))
    full = lambda a: pl.BlockSpec(a.shape, lambda b, s, pt: (0,) * a.ndim)

    def page_spec(j):
        return pl.BlockSpec((None, None, CACHE_ROW, PAGE_SIZE),
                            lambda b, s, pt: (layer, pt[b * n_pages + s * PAGES_PER_STEP + j], 0, 0))

    grid_spec = pltpu.PrefetchScalarGridSpec(
        num_scalar_prefetch=1,
        grid=(nb, n_steps),
        in_specs=[per_b(qa), per_b(qr), per_b(sself), per_b(lat_new), full(w_ukt)]
        + [page_spec(j) for j in range(PAGES_PER_STEP)],
        out_specs=pl.BlockSpec((1, N_HEADS, KV_LORA_RANK), lambda b, s, pt: (b, 0, 0)),
        scratch_shapes=[pltpu.VMEM((N_HEADS, 1), F32), pltpu.VMEM((N_HEADS, 1), F32),
                        pltpu.VMEM((N_HEADS, KV_LORA_RANK), F32),
                        pltpu.VMEM((KV_LORA_RANK, PAGES_PER_STEP * PAGE_SIZE), BF16)],
    )
    return pl.pallas_call(
        _sample_attn_kernel,
        grid_spec=grid_spec,
        out_shape=jax.ShapeDtypeStruct((nb, N_HEADS, KV_LORA_RANK), F32),
        compiler_params=_cparams(2),
        name="sample_attention",
    )(page_table.reshape(-1), qa, qr, sself, lat_new, w_ukt, *([cache_t] * PAGES_PER_STEP))


def _sample_out_kernel(ctx_ref, x_ref, gate_ref, w_uv_ref, w_out_ref, o_ref):
    o_ref[...] = _attn_out(lambda hd: ctx_ref[:, hd * KV_LORA_RANK:(hd + 1) * KV_LORA_RANK],
                           x_ref[...], gate_ref[...], w_uv_ref, w_out_ref)


def _sample_out(ctx, x, gate, w_uv, w_out):
    return pl.pallas_call(
        _sample_out_kernel,
        out_shape=jax.ShapeDtypeStruct(x.shape, F32),
        compiler_params=pltpu.CompilerParams(vmem_limit_bytes=VMEM_LIMIT),
        name="sample_attention_out",
    )(ctx, x, gate, w_uv, w_out)


def _pool_mix(window_sum, h, cnt, w_pool_ref, pscale):
    parts = []
    for gi, w in enumerate(POOL_WINDOWS):
        ch = slice(gi * POOL_GROUP_DIM, (gi + 1) * POOL_GROUP_DIM)
        acc = h[:, ch]
        for kk in range(1, w):
            acc = acc + window_sum(kk, ch)
        d = acc / cnt[gi] - h[:, ch]
        parts.append(_bdot(d, w_pool_ref[gi]))
    return jnp.concatenate(parts, axis=-1) * pscale


def _pool_prompt_kernel(x_ref, shift_ref, scale_ref, gate_ref, gmix_ref, w_pool_ref, pscale_ref,
                        o_ref, st_ref, ext_s):
    si = pl.program_id(1)
    t = x_ref.shape[1]

    @pl.when(si == 0)
    def _():
        ext_s[0:POOL_HALO] = jnp.zeros((POOL_HALO, D_MODEL), F32)

    @pl.when(si > 0)
    def _():
        ext_s[0:POOL_HALO] = ext_s[t:t + POOL_HALO]

    x = x_ref[0]
    h = _rms(x, gmix_ref[...]) * (1.0 + scale_ref[0]) + shift_ref[0]
    ext_s[POOL_HALO:POOL_HALO + t] = h
    pos = si * t + lax.broadcasted_iota(jnp.int32, (t, 1), 0)
    cnt = [jnp.minimum(pos + 1, w).astype(F32) for w in POOL_WINDOWS]
    y = _pool_mix(lambda kk, ch: ext_s[POOL_HALO - kk:POOL_HALO - kk + t, ch], h, cnt,
                  w_pool_ref, pscale_ref[...])
    o_ref[0] = x + gate_ref[0] * y

    @pl.when(si == pl.num_programs(1) - 1)
    def _():
        st_ref[0] = ext_s[POOL_HALO + t - POOL_STATE_LEN:POOL_HALO + t]


def _pool_prompt(x, shift, scale, gate, gmix, w_pool, pscale, tile):
    nb, s, _ = x.shape
    tok = pl.BlockSpec((1, tile, D_MODEL), lambda b, i: (b, i, 0))
    mod = pl.BlockSpec((1, 1, D_MODEL), lambda b, i: (b, 0, 0))
    full = lambda a: pl.BlockSpec(a.shape, lambda b, i: (0,) * a.ndim)
    return pl.pallas_call(
        _pool_prompt_kernel,
        grid=(nb, s // tile),
        in_specs=[tok, mod, mod, mod, full(gmix), full(w_pool), full(pscale)],
        out_specs=[tok, pl.BlockSpec((1, POOL_STATE_LEN, D_MODEL), lambda b, i: (b, 0, 0))],
        out_shape=[jax.ShapeDtypeStruct(x.shape, F32),
                   jax.ShapeDtypeStruct((nb, POOL_STATE_LEN, D_MODEL), F32)],
        scratch_shapes=[pltpu.VMEM((POOL_HALO + tile, D_MODEL), F32)],
        compiler_params=_cparams(2),
        name="pool_prompt",
    )(x, shift, scale, gate, gmix, w_pool, pscale)


def _pool_sample_kernel(x_ref, shift_ref, scale_ref, gate_ref, gmix_ref, w_pool_ref, pscale_ref,
                        state_ref, o_ref, st_ref):
    x = x_ref[...]
    h = _rms(x, gmix_ref[...]) * (1.0 + scale_ref[...]) + shift_ref[...]
    cnt = [float(w) for w in POOL_WINDOWS]
    y = _pool_mix(lambda kk, ch: state_ref[POOL_STATE_LEN - kk, :, ch], h, cnt,
                  w_pool_ref, pscale_ref[...])
    o_ref[...] = x + gate_ref[...] * y
    for r in range(POOL_STATE_LEN - 1):
        st_ref[r] = state_ref[r + 1]
    st_ref[POOL_STATE_LEN - 1] = h


def _pool_sample(x, shift, scale, gate, gmix, w_pool, pscale, state_t):
    return pl.pallas_call(
        _pool_sample_kernel,
        out_shape=[jax.ShapeDtypeStruct(x.shape, F32), jax.ShapeDtypeStruct(state_t.shape, F32)],
        compiler_params=pltpu.CompilerParams(vmem_limit_bytes=VMEM_LIMIT),
        name="pool_sample",
    )(x, shift, scale, gate, gmix, w_pool, pscale, state_t)


NOT_TOP = 64.0
LANES = 128
SUBLANES = 8


def _sort_pairs(n):
    pairs, p = [], 1
    while p < n:
        k = p
        while k >= 1:
            for j in range(k % p, n - k, 2 * k):
                for i in range(min(k, n - j - k)):
                    if (i + j) // (2 * p) == (i + j + k) // (2 * p):
                        pairs.append((i + j, i + j + k))
            k //= 2
        p *= 2
    return pairs


def _bitonic_merge_pairs(n):
    pairs, k = [], n // 2
    while k >= 1:
        pairs += [(i, i + k) for i in range(n) if not i & k]
        k //= 2
    return pairs


def _compare_exchange(v, pairs):
    for i, j in pairs:
        v[i], v[j] = jnp.maximum(v[i], v[j]), jnp.minimum(v[i], v[j])
    return v


def _top16_sorted(s):
    v = [s[r * SUBLANES:(r + 1) * SUBLANES] for r in range(PEER_TOPK)]
    v = _compare_exchange(v, _sort_pairs(PEER_TOPK))
    for shift in (4, 2, 1):
        v = [jnp.maximum(v[i], pltpu.roll(v[PEER_TOPK - 1 - i], shift, axis=0)) for i in range(PEER_TOPK)]
        v = _compare_exchange(v, _bitonic_merge_pairs(PEER_TOPK))
    return v


def _kth_largest(work, k):
    for _ in range(k):
        m = jnp.max(work, axis=0, keepdims=True)
        work = jnp.where(work == m, NEG_INF, work)
    return m


def _prefix_count(x, thresholds, above):
    th = thresholds
    parts = []
    for r in range(x.shape[0] // SUBLANES):
        xr = x[r * SUBLANES:(r + 1) * SUBLANES]
        passes = (lambda t: t > xr) if above else (lambda t: xr >= t)
        c8 = passes(th[7])
        c4 = passes(jnp.where(c8, th[11], th[3]))
        c2 = passes(jnp.where(c8, jnp.where(c4, th[13], th[9]), jnp.where(c4, th[5], th[1])))
        c1 = passes(jnp.where(c8, jnp.where(c4, jnp.where(c2, th[14], th[12]), jnp.where(c2, th[10], th[8])),
                              jnp.where(c4, jnp.where(c2, th[6], th[4]), jnp.where(c2, th[2], th[0]))))
        cnt = (jnp.where(c8, 8.0, 0.0) + jnp.where(c4, 4.0, 0.0)) + (jnp.where(c2, 2.0, 0.0) + jnp.where(c1, 1.0, 0.0))
        parts.append(jnp.where(passes(th[15]), float(PEER_TOPK), cnt))
    return jnp.concatenate(parts, axis=0)


def _dup_bf16(x):
    hi = pltpu.bitcast(x.astype(BF16).astype(F32), jnp.uint32)
    return hi | (hi >> 16)


def _pack_pairs(x):
    return pltpu.bitcast(x.astype(BF16), jnp.uint32)


def _peer_route_kernel(x_ref, shift_ref, scale_ref, gffn_ref, wqt_ref, subk_ref,
                       ht_ref, r2_ref, e2_ref, n1_ref, e1_ref):
    t = x_ref.shape[1]
    h = _rms(x_ref[0], gffn_ref[...]) * (1.0 + scale_ref[0]) + shift_ref[0]
    ht = h.T.astype(BF16)
    ht_ref[...] = pltpu.bitcast(ht, jnp.uint32)
    qt = jnp.dot(wqt_ref[...], ht, preferred_element_type=F32).astype(BF16)
    row8 = lax.broadcasted_iota(jnp.int32, (SUBLANES, LANES), 0)
    for hd in range(PEER_HEADS):
        sc = [jnp.dot(subk_ref[2 * hd + p], qt[(2 * hd + p) * PEER_HALF:(2 * hd + p + 1) * PEER_HALF],
                      preferred_element_type=F32) for p in range(2)]
        for lt in range(t // LANES):
            ln = slice(lt * LANES, (lt + 1) * LANES)
            s1, s2 = sc[0][:, ln], sc[1][:, ln]
            top1, top2 = _top16_sorted(s1), _top16_sorted(s2)
            v1_lo, v1_hi = top1[0], top1[SUBLANES]
            for a in range(1, SUBLANES):
                v1_lo = jnp.where(row8 == a, top1[a], v1_lo)
                v1_hi = jnp.where(row8 == a, top1[SUBLANES + a], v1_hi)
            t1, t2 = top1, top2
            below = lambda r: row8 < r
            cand = jnp.concatenate([
                v1_lo + t2[0],
                v1_hi + t2[0],
                v1_lo + t2[1],
                jnp.where(below(5), v1_lo, jnp.where(row8 == 5, t1[0], jnp.where(row8 == 6, t1[1], t1[2])))
                + jnp.where(below(5), t2[2], t2[4]),
                jnp.where(below(4), v1_lo, jnp.where((row8 & 1) == 0, t1[0], t1[1]))
                + jnp.where(below(4), t2[3], jnp.where(below(6), t2[5], t2[6])),
                jnp.where(below(2), v1_lo, t1[0])
                + jnp.where(below(2), t2[7], jnp.where(below(3), t2[8], jnp.where(below(4), t2[9], jnp.where(
                    below(5), t2[10], jnp.where(below(6), t2[11], jnp.where(below(7), t2[12], t2[13])))))),
                t1[0] + jnp.where(below(1), t2[14], jnp.where(below(2), t2[15], NEG_INF)),
            ], axis=0)
            tau = _kth_largest(cand, PEER_TOPK)
            cmax = top1[0] + top2[0]
            z = jnp.sum(jnp.where(cand >= tau, jnp.exp(cand - cmax[:1]), 0.0), axis=0, keepdims=True)
            theta = []
            for b in range(PEER_TOPK):
                if b >= SUBLANES:
                    theta.append(jnp.where(t1[0] + t2[b] >= tau, t1[0], -NEG_INF)[:1])
                    continue
                lo = jnp.where(v1_lo + t2[b] >= tau, v1_lo, -NEG_INF)
                if b == 0:
                    lo = jnp.minimum(lo, jnp.where(v1_hi + t2[b] >= tau, v1_hi, -NEG_INF))
                theta.append(jnp.min(lo, axis=0, keepdims=True))
            n1 = _prefix_count(s1, theta, above=False)
            rank2 = _prefix_count(s2, top2, above=True)
            rank2 = jnp.where(rank2 >= PEER_TOPK, NOT_TOP, rank2)
            r2_ref[hd, :, ln] = _pack_pairs(rank2)
            e2_ref[hd, :, ln] = _pack_pairs(jnp.exp(s2 - top2[0][:1]))
            n1_ref[hd, lt] = _dup_bf16(n1)
            e1_ref[hd, lt] = _dup_bf16(jnp.exp(s1 - top1[0][:1]) * (0.5 / z))


def _peer_route(x, shift, scale, gffn, wqt, subk, tile):
    nb, s, _ = x.shape
    r = shift.shape[1]
    ns = s // tile
    n = nb * s
    rt = tile if r == s else 1
    tok = pl.BlockSpec((1, tile, D_MODEL), lambda b, i: (b, i, 0))
    mod = pl.BlockSpec((1, rt, D_MODEL), (lambda b, i: (b, i, 0)) if r == s else (lambda b, i: (b, 0, 0)))
    full = lambda a: pl.BlockSpec(a.shape, lambda b, i: (0,) * a.ndim)
    u32 = jnp.uint32
    lane_tiles = tile // LANES
    pairs = pl.BlockSpec((PEER_HEADS, PEER_N_KEYS // 2, tile), lambda b, i: (0, 0, b * ns + i))
    rows = pl.BlockSpec((PEER_HEADS, lane_tiles, PEER_N_KEYS, LANES), lambda b, i: (0, b * ns + i, 0, 0))
    pairs_shape = jax.ShapeDtypeStruct((PEER_HEADS, PEER_N_KEYS // 2, n), u32)
    rows_shape = jax.ShapeDtypeStruct((PEER_HEADS, n // LANES, PEER_N_KEYS, LANES), u32)
    return pl.pallas_call(
        _peer_route_kernel,
        grid=(nb, ns),
        in_specs=[tok, mod, mod, full(gffn), full(wqt), full(subk)],
        out_specs=[pl.BlockSpec((D_MODEL // 2, tile), lambda b, i: (0, b * ns + i)),
                   pairs, pairs, rows, rows],
        out_shape=[jax.ShapeDtypeStruct((D_MODEL // 2, n), u32), pairs_shape, pairs_shape,
                   rows_shape, rows_shape],
        compiler_params=_cparams(2),
        name="peer_route",
    )(x, shift, scale, gffn, wqt, subk)


N_CHUNKS = PEER_N_EXPERTS // EXPERT_CHUNK
GATE_ROWS = 32
OUT_DEPTH = 1024
OUT_ROWS = 1024
KEYS_PER_PIECE = 8


def _peer_sweep_kernel(ht_ref, r2_ref, e2_ref, n1_ref, e1_ref, u_ref, vt_ref, x_ref, gate_ref, o_ref,
                       ga_s, yt_s):
    j = pl.program_id(2)
    t = ht_ref.shape[1]

    @pl.when(j == 0)
    def _():
        yt_s[...] = jnp.zeros(yt_s.shape, F32)

    ht = pltpu.bitcast(ht_ref[...], BF16)
    zero = jnp.zeros((GATE_ROWS, LANES), BF16)
    n_groups = PEER_N_KEYS // GATE_ROWS

    def gate_piece(k):
        n_rows = KEYS_PER_PIECE * PEER_N_KEYS
        u = pltpu.bitcast(u_ref[k * n_rows // 2:(k + 1) * n_rows // 2, :], BF16)
        act = jnp.dot(u, ht, preferred_element_type=F32)
        ga_s[k * n_rows:(k + 1) * n_rows, :] = (act * (1.0 + lax.erf(act * (2.0 ** -0.5)))).astype(BF16)
        for lt in range(t // LANES):
            ln = slice(lt * LANES, (lt + 1) * LANES)
            g = [[zero] * n_groups for _ in range(KEYS_PER_PIECE)]
            for hd in range(PEER_HEADS):
                n1, e1 = [], []
                for kk in range(KEYS_PER_PIECE):
                    row = pl.ds(j * KEY_ROWS_PER_CHUNK + k * KEYS_PER_PIECE + kk, 1)
                    spread = lambda ref: pltpu.bitcast(
                        jnp.broadcast_to(ref[hd, lt, row, :], (GATE_ROWS // 2, LANES)), BF16)
                    n1.append(spread(n1_ref))
                    e1.append(spread(e1_ref))
                for sg in range(n_groups):
                    pairs = slice(sg * GATE_ROWS // 2, (sg + 1) * GATE_ROWS // 2)
                    r2 = pltpu.bitcast(r2_ref[hd, pairs, ln], BF16)
                    e2 = pltpu.bitcast(e2_ref[hd, pairs, ln], BF16)
                    for kk in range(KEYS_PER_PIECE):
                        g[kk][sg] = g[kk][sg] + jnp.where(r2 < n1[kk], e2 * e1[kk], zero)
            for kk in range(KEYS_PER_PIECE):
                for sg in range(n_groups):
                    r0 = k * n_rows + kk * PEER_N_KEYS + sg * GATE_ROWS
                    ga_s[r0:r0 + GATE_ROWS, ln] = ga_s[r0:r0 + GATE_ROWS, ln] * g[kk][sg]

    def out_piece(group, m):
        cols = slice(group * OUT_DEPTH, (group + 1) * OUT_DEPTH)
        rows = slice(m * OUT_ROWS, (m + 1) * OUT_ROWS)
        v = pltpu.bitcast(vt_ref[m * OUT_ROWS // 2:(m + 1) * OUT_ROWS // 2, cols], BF16)
        yt_s[rows, :] += jnp.dot(v, ga_s[cols, :], preferred_element_type=F32)

    pieces_per_group = OUT_DEPTH // (KEYS_PER_PIECE * PEER_N_KEYS)
    n_pieces = KEY_ROWS_PER_CHUNK // KEYS_PER_PIECE
    pending = []
    for k in range(n_pieces):
        gate_piece(k)
        if pending:
            out_piece(*pending.pop(0))
        if (k + 1) % pieces_per_group == 0:
            pending += [(k // pieces_per_group, m) for m in range(D_MODEL // OUT_ROWS)]
    for piece in pending:
        out_piece(*piece)

    @pl.when(j == pl.num_programs(2) - 1)
    def _():
        o_ref[0] = x_ref[0] + gate_ref[0] * yt_s[...].T


def _peer_sweep(ht, r2, e2, n1, e1, u, vt, x, gate, tile):
    nb, s, _ = x.shape
    r = gate.shape[1]
    ns = s // tile
    pairs = pl.BlockSpec((PEER_HEADS, PEER_N_KEYS // 2, tile), lambda b, i, j: (0, 0, b * ns + i))
    rows = pl.BlockSpec((PEER_HEADS, tile // LANES, PEER_N_KEYS, LANES),
                        lambda b, i, j: (0, b * ns + i, 0, 0))
    tok = pl.BlockSpec((1, tile, D_MODEL), lambda b, i, j: (b, i, 0))
    mod = pl.BlockSpec((1, tile if r == s else 1, D_MODEL),
                       (lambda b, i, j: (b, i, 0)) if r == s else (lambda b, i, j: (b, 0, 0)))
    return pl.pallas_call(
        _peer_sweep_kernel,
        grid=(nb, ns, N_CHUNKS),
        in_specs=[pl.BlockSpec((D_MODEL // 2, tile), lambda b, i, j: (0, b * ns + i)),
                  pairs, pairs, rows, rows,
                  pl.BlockSpec((EXPERT_CHUNK // 2, D_MODEL), lambda b, i, j: (j, 0)),
                  pl.BlockSpec((D_MODEL // 2, EXPERT_CHUNK), lambda b, i, j: (0, j)),
                  tok, mod],
        out_specs=tok,
        out_shape=jax.ShapeDtypeStruct(x.shape, F32),
        scratch_shapes=[pltpu.VMEM((EXPERT_CHUNK, tile), BF16), pltpu.VMEM((D_MODEL, tile), F32)],
        compiler_params=_cparams(3),
        name="peer_sweep",
    )(ht, r2, e2, n1, e1, u, vt, x, gate)


def _peer(x, shift, scale, gate, gffn, pw, route_tile, sweep_tile):
    routed = _peer_route(x, shift, scale, gffn, pw["wqt"], pw["subk"], route_tile)
    return _peer_sweep(*routed, pw["u"], pw["vt"], x, gate, sweep_tile)


def _rope_tables(pos):
    inv = ROPE_THETA ** (-jnp.arange(0, QK_ROPE_DIM, 2, dtype=F32) / QK_ROPE_DIM)
    ang = pos.astype(F32)[:, None] * inv[None, :]
    c, s, z = jnp.cos(ang), jnp.sin(ang), jnp.zeros_like(ang)
    return jnp.concatenate([c, z, c, z], -1), jnp.concatenate([-s, z, s, z], -1)


def _pad_rope_cols(w):
    z = jnp.zeros(w.shape[:-1] + (ROPE_HALF,), w.dtype)
    return jnp.concatenate([w[..., :ROPE_HALF], z, w[..., ROPE_HALF:], z], -1)


def _mla_weights(w_in, g_q_lat, g_kv_lat, w_uq, w_uk, w_uv, g_qn_nope, g_qn_rope, g_kn_nope,
                 g_kn_rope, w_out):
    lat_end = Q_LORA_RANK + KV_LORA_RANK
    w_in_p = jnp.concatenate([w_in[:, :lat_end], _pad_rope_cols(w_in[:, lat_end:])], -1)
    wq = w_uq.reshape(Q_LORA_RANK, N_HEADS, QK_NOPE_DIM + QK_ROPE_DIM)
    wq = jnp.concatenate([wq[..., :QK_NOPE_DIM], _pad_rope_cols(wq[..., QK_NOPE_DIM:])], -1)
    row = lambda g: g.reshape(1, -1)
    return {
        "w_in": w_in_p.astype(BF16),
        "g_q_lat": row(g_q_lat), "g_kv_lat": row(g_kv_lat), "g_kr": row(_pad_rope_cols(g_kn_rope)),
        "w_uq": wq.reshape(Q_LORA_RANK, N_HEADS * HEAD_PAD).astype(BF16),
        "g_qn": row(g_qn_nope), "g_qr": row(_pad_rope_cols(g_qn_rope)),
        "w_uk": w_uk.reshape(KV_LORA_RANK, N_HEADS * QK_NOPE_DIM).astype(BF16),
        "g_kn": row(g_kn_nope),
        "w_ukt": jnp.transpose(w_uk, (1, 2, 0)).astype(BF16),
        "w_uv": jnp.transpose(w_uv, (1, 0, 2)).astype(BF16),
        "w_out": w_out.astype(BF16),
    }


def _pack_rows_kernel(x_ref, o_ref):
    o_ref[...] = _pack_pairs(x_ref[...])


def _pack_cols_kernel(x_ref, o_ref):
    o_ref[...] = _pack_pairs(x_ref[...].T)


def _pack_weight(w, layer, transpose):
    _, r, c = w.shape
    blk = 2048
    src = pl.BlockSpec((None, blk, c), lambda i: (layer, i, 0))
    if transpose:
        return pl.pallas_call(
            _pack_cols_kernel, grid=(r // blk,), in_specs=[src],
            out_specs=pl.BlockSpec((c // 2, blk), lambda i: (0, i)),
            out_shape=jax.ShapeDtypeStruct((c // 2, r), jnp.uint32),
            compiler_params=_cparams(1), name="pack_weight_t")(w)
    return pl.pallas_call(
        _pack_rows_kernel, grid=(r // blk,), in_specs=[src],
        out_specs=pl.BlockSpec((blk // 2, c), lambda i: (i, 0)),
        out_shape=jax.ShapeDtypeStruct((r // 2, c), jnp.uint32),
        compiler_params=_cparams(1), name="pack_weight")(w)


def _peer_weights(w_q, subkeys, u, v, layer):
    return {
        "wqt": w_q[layer].T.astype(BF16),
        "subk": subkeys[layer].reshape(2 * PEER_HEADS, PEER_N_KEYS, PEER_HALF).astype(BF16),
        "u": _pack_weight(u, layer, False),
        "vt": _pack_weight(v, layer, True),
    }


PROMPT_TILE = 256
SAMPLE_TILE = 128
PEER_SWEEP_TILE = 512
ATTN_TILE = 512
POOL_TILE = 512
Q_SUB = 256


def kernel(x_prompt, x_sample, c_prompt, c_sample, cache_mla, state_pool, page_table, norm_mix, norm_ffn, w_ada, b_ada, w_mla_in, g_q_lat, g_kv_lat, w_uq, w_uk, w_uv, g_qn_nope, g_qn_rope, g_kn_nope, g_kn_rope, w_mla_out, w_pool, pool_scale, w_peer_q, peer_subkeys, peer_u, peer_v):
    nbp, seq, _ = x_prompt.shape
    nbs = x_sample.shape[0]
    depth = w_ada.shape[0]
    xp = x_prompt
    xs = x_sample.reshape(1, nbs, D_MODEL)

    mods = _ada_params(jnp.concatenate([c_prompt, c_sample], 0), w_ada, b_ada)
    mods = mods.reshape(depth, nbp + nbs, 6, D_MODEL)
    cs_p, sn_p = _rope_tables(jnp.arange(seq))
    cs_s, sn_s = _rope_tables(jnp.full((nbs,), PAST_LEN))
    cache_t = jnp.swapaxes(cache_mla, 2, 3)

    rows_p, rows_s, pst_p, pst_s = [], [], [], []
    for l in range(depth):
        mp = [mods[l, :nbp, k].reshape(nbp, 1, D_MODEL) for k in range(6)]
        ms = [mods[l, nbp:, k].reshape(1, nbs, D_MODEL) for k in range(6)]
        gmix = norm_mix[l].reshape(1, D_MODEL)
        gffn = norm_ffn[l].reshape(1, D_MODEL)
        i = l // 2
        if l % 2 == 0:
            mw = _mla_weights(w_mla_in[i], g_q_lat[i], g_kv_lat[i], w_uq[i], w_uk[i], w_uv[i],
                              g_qn_nope[i], g_qn_rope[i], g_kn_nope[i], g_kn_rope[i], w_mla_out[i])
            r_p, q, k, v = _mla_project(xp, mp[0], mp[1], gmix, cs_p, sn_p, mw, PROMPT_TILE, False)
            xp = _prompt_attention(q, k, v, xp, mp[2], mw["w_uv"], mw["w_out"], ATTN_TILE)
            r_s, q, k, v, qa, sself = _mla_project(xs, ms[0], ms[1], gmix, cs_s, sn_s, mw,
                                                   SAMPLE_TILE, True)
            q = q.reshape(nbs, N_HEADS, HEAD_PAD)[:, :, QK_NOPE_DIM:]
            qr = jnp.concatenate([q[..., :ROPE_HALF], q[..., 2 * ROPE_HALF:3 * ROPE_HALF]], -1)
            qa = jnp.pad(qa.reshape(nbs, N_HEADS, KV_LORA_RANK), ((0, 0), (0, QA_ROWS - N_HEADS), (0, 0)))
            ctx = _sample_attention(
                cache_t, i, page_table, qa, qr, sself.reshape(nbs, 128)[:, :N_HEADS, None],
                r_s.reshape(nbs, 1, CACHE_ROW)[..., :KV_LORA_RANK],
                mw["w_ukt"].reshape(N_HEADS * QK_NOPE_DIM, KV_LORA_RANK))
            xs = _sample_out(ctx.reshape(nbs, N_HEADS * KV_LORA_RANK), xs[0], ms[2][0],
                             mw["w_uv"], mw["w_out"]).reshape(1, nbs, D_MODEL)
            rows_p.append(r_p)
            rows_s.append(r_s.reshape(nbs, 1, CACHE_ROW))
        else:
            wp = w_pool[i].astype(BF16)
            psc = pool_scale[i].reshape(1, D_MODEL)
            xp, st_p = _pool_prompt(xp, mp[0], mp[1], mp[2], gmix, wp, psc, POOL_TILE)
            x2, st_s = _pool_sample(xs[0], ms[0][0], ms[1][0], ms[2][0], gmix, wp, psc,
                                    jnp.transpose(state_pool[i], (1, 0, 2)))
            xs = x2.reshape(1, nbs, D_MODEL)
            pst_p.append(st_p)
            pst_s.append(jnp.transpose(st_s, (1, 0, 2)))
        pw = _peer_weights(w_peer_q, peer_subkeys, peer_u, peer_v, l)
        xp = _peer(xp, mp[3], mp[4], mp[5], gffn, pw, PROMPT_TILE, PEER_SWEEP_TILE)
        xs = _peer(xs, ms[3], ms[4], ms[5], gffn, pw, SAMPLE_TILE, SAMPLE_TILE)
    return (xp, xs.reshape(nbs, 1, D_MODEL), jnp.stack(rows_p), jnp.stack(rows_s),
            jnp.stack(pst_p), jnp.stack(pst_s))
```
